```python
import math
import jax
import jax.numpy as jnp
from jax import lax
import numpy as np

D_MODEL = 2048
BATCH = 2
SEQ = 4096
DEPTH = 2
DEC_BATCH = 128
DEC_SEQ = 4
PAST_LEN = 2048
PAGE_SIZE = 128

HEAD_DIM = 128
A_HEADS = 8
A_PATTERNS = ((128, 1), (512, 4), (2048, 16))
A_WMAX = max(w for w, _ in A_PATTERNS)
A_QBLOCK = 128
B_HEADS = 8
B_DK = 128
B_DV = 128
B_CONV = 4
DN_CHUNK = 64
C_HEADS = 4
C_DK = D_MODEL // (2 * C_HEADS)
C_DV = D_MODEL // C_HEADS
C_GATE_RANK = 16
C_GATE_TAU = 16.0
GLA_CHUNK = 64
D_FF = 4 * D_MODEL
FFN_CONV = 3
PLE_DIM = 256
EPS = 1e-6
N_EVEN = (DEPTH + 1) // 2
N_ODD = DEPTH // 2
A_QKV = 3 * A_HEADS * HEAD_DIM
B_QK = B_HEADS * B_DK
B_V = B_HEADS * B_DV
B_QKV = 2 * B_QK + B_V
N_IN_EVEN = A_QKV + B_QKV + 2 * B_HEADS + B_V
N_MIX_EVEN = A_HEADS * HEAD_DIM + B_V
C_QK = C_HEADS * C_DK
C_V = C_HEADS * C_DV
N_IN_ODD = 2 * C_QK + C_V + C_GATE_RANK + C_V

kernel_name = 'hybrid_dilated_delta_gla_decode_step'


def _rmsnorm(x, w):
    xf = x.astype(jnp.float32)
    y = xf * lax.rsqrt(jnp.mean(xf * xf, axis=-1, keepdims=True) + EPS)
    return (y * w.astype(jnp.float32)).astype(x.dtype)


def _l2norm(x):
    return x * lax.rsqrt(jnp.sum(x * x, axis=-1, keepdims=True) + EPS)


def _split_cols(x, sizes):
    idx, acc = [], 0
    for s in sizes[:-1]:
        acc += s
        idx.append(acc)
    return jnp.split(x, idx, axis=-1)


def _causal_dwconv(x, buf, w):
    t = x.shape[1]
    width = w.shape[0]
    xp = jnp.concatenate([buf.astype(x.dtype), x], axis=1)
    y = w[0] * xp[:, 0:t]
    for i in range(1, width):
        y = y + w[i] * xp[:, i:i + t]
    return y, xp[:, xp.shape[1] - (width - 1):]


def _to_chunks(a, c):
    n, t, h = a.shape[:3]
    a = a.reshape((n, t // c, c, h) + a.shape[3:])
    return a.transpose((1, 0, 3, 2) + tuple(range(4, a.ndim)))


def _from_chunks(o):
    nc, n, h, c, d = o.shape
    return o.transpose(1, 0, 3, 2, 4).reshape(n, nc * c, h, d)


def _dilated_window_attention(q, k_src, v_src, offset):
    n, t, h, dh = q.shape
    qb = math.gcd(t, A_QBLOCK)
    nb = t // qb
    scale = dh ** -0.5
    q_blocks = q.reshape(n, nb, qb, h, dh).transpose(1, 0, 2, 3, 4)
    q_rows = (offset + jnp.arange(t, dtype=jnp.int32)).reshape(nb, qb)

    def block(args):
        qblk, qrow = args
        outs, lses = [], []
        for window, dil in A_PATTERNS:
            n_keys = window // dil + 1
            kidx = qrow[:, None] - dil * jnp.arange(n_keys, dtype=jnp.int32)[None, :]
            valid = kidx >= 0
            kidx = jnp.maximum(kidx, 0)
            kg = jnp.take(k_src, kidx, axis=1)
            vg = jnp.take(v_src, kidx, axis=1)
            s = jnp.einsum('nqhd,nqkhd->nhqk', qblk, kg).astype(jnp.float32) * scale
            s = jnp.where(valid[None, None], s, -jnp.inf)
            lse = jax.nn.logsumexp(s, axis=-1)
            pr = jnp.exp(s - lse[..., None])
            outs.append(jnp.einsum('nhqk,nqkhd->nqhd', pr.astype(vg.dtype), vg).astype(jnp.float32))
            lses.append(lse)
        wts = jax.nn.softmax(jnp.stack(lses, axis=0), axis=0).transpose(0, 1, 3, 2)[..., None]
        return jnp.sum(jnp.stack(outs, axis=0) * wts, axis=0).astype(q.dtype)

    out = lax.map(block, (q_blocks, q_rows))
    return out.transpose(1, 0, 2, 3, 4).reshape(n, t, h, dh)


def _gated_delta_rule(q, k, v, beta, g, s0):
    t = q.shape[1]
    dv = v.shape[-1]
    c = math.gcd(t, DN_CHUNK)
    qc, kc, vc, bc, gc = [_to_chunks(a, c) for a in (q, k, v, beta, g)]
    gcum = jnp.cumsum(gc, axis=-1)
    idx = jnp.arange(c)
    incl = idx[:, None] >= idx[None, :]
    strict = idx[:, None] > idx[None, :]
    decay = jnp.exp(jnp.where(incl, gcum[..., :, None] - gcum[..., None, :], -jnp.inf))
    kbeta = kc * bc[..., None]
    lower = jnp.where(strict, jnp.einsum('...id,...jd->...ij', kbeta, kc) * decay, 0.0)
    a_mat = lower + jnp.eye(c, dtype=lower.dtype)
    rhs = jnp.concatenate([vc * bc[..., None], kbeta * jnp.exp(gcum)[..., None]], axis=-1)
    sol = lax.linalg.triangular_solve(a_mat, rhs, left_side=True, lower=True, unit_diagonal=True)
    u, w = sol[..., :dv], sol[..., dv:]
    attn = jnp.einsum('...id,...jd->...ij', qc, kc) * decay
    qg = qc * jnp.exp(gcum)[..., None]
    kd = kc * jnp.exp(gcum[..., -1:] - gcum)[..., None]
    glast = jnp.exp(gcum[..., -1])[..., None, None]

    def step(s, xs):
        qg_c, kd_c, u_c, w_c, at_c, gl_c = xs
        v_new = u_c - w_c @ s
        o = qg_c @ s + at_c @ v_new
        s = s * gl_c + jnp.swapaxes(kd_c, -1, -2) @ v_new
        return s, o

    s_fin, o = lax.scan(step, s0, (qg, kd, u, w, attn, glast))
    return _from_chunks(o), s_fin


def _gla(q, k, v, log_a, s0):
    t = q.shape[1]
    c = math.gcd(t, GLA_CHUNK)
    xs = tuple(_to_chunks(a, c) for a in (q, k, v, log_a))
    idx = jnp.arange(c)
    incl = (idx[:, None] >= idx[None, :])[..., None]

    def step(s, xs_c):
        qc, kc, vc, gc = xs_c
        gcum = jnp.cumsum(gc, axis=-2)
        decay = jnp.exp(jnp.where(incl, gcum[..., :, None, :] - gcum[..., None, :, :], -jnp.inf))
        attn = jnp.einsum('nhid,nhjd,nhijd->nhij', qc, kc, decay)
        o = (qc * jnp.exp(gcum)) @ s + attn @ vc
        glast = gcum[..., -1:, :]
        s = s * jnp.exp(jnp.swapaxes(glast, -1, -2)) + jnp.swapaxes(kc * jnp.exp(glast - gcum), -1, -2) @ vc
        return s, o

    s_fin, o = lax.scan(step, s0, xs)
    return _from_chunks(o), s_fin


def _mixer_even(h, buf_k, buf_v, conv_buf, dn_s, w_in, w_out, conv_w, a_log, dt_bias, norm_w):
    n, t, _ = h.shape
    f32 = jnp.float32
    a_qkv, b_qkv, b_beta, b_a, b_z = _split_cols(h @ w_in, (A_QKV, B_QKV, B_HEADS, B_HEADS, B_V))
    qa, ka, va = [z.reshape(n, t, A_HEADS, HEAD_DIM) for z in jnp.split(a_qkv, 3, axis=-1)]
    k_src = jnp.concatenate([buf_k.astype(ka.dtype), ka], axis=1)
    v_src = jnp.concatenate([buf_v.astype(va.dtype), va], axis=1)
    o_a = _dilated_window_attention(qa, k_src, v_src, buf_k.shape[1])
    cq, new_conv = _causal_dwconv(b_qkv, conv_buf, conv_w)
    cq = jax.nn.silu(cq).astype(f32)
    qb, kb, vb = _split_cols(cq, (B_QK, B_QK, B_V))
    qb = _l2norm(qb.reshape(n, t, B_HEADS, B_DK)) * (B_DK ** -0.5)
    kb = _l2norm(kb.reshape(n, t, B_HEADS, B_DK))
    vb = vb.reshape(n, t, B_HEADS, B_DV)
    beta = jax.nn.sigmoid(b_beta.astype(f32))
    g = -jnp.exp(a_log.astype(f32)) * jax.nn.softplus(b_a.astype(f32) + dt_bias.astype(f32))
    o_b, s_new = _gated_delta_rule(qb, kb, vb, beta, g, dn_s.astype(f32))
    o_b = _rmsnorm(o_b.astype(h.dtype), norm_w) * jax.nn.silu(b_z.reshape(n, t, B_HEADS, B_DV))
    o = jnp.concatenate([o_a.reshape(n, t, A_HEADS * HEAD_DIM), o_b.reshape(n, t, B_V)], axis=-1) @ w_out
    return o, ka, va, new_conv, s_new.astype(dn_s.dtype)


def _mixer_odd(h, gla_s, w_in, gate_w2, gate_b, norm_w, w_out):
    n, t, _ = h.shape
    f32 = jnp.float32
    q, k, v, glr, rg = _split_cols(h @ w_in, (C_QK, C_QK, C_V, C_GATE_RANK, C_V))
    q = q.astype(f32).reshape(n, t, C_HEADS, C_DK) * (C_DK ** -0.5)
    k = k.astype(f32).reshape(n, t, C_HEADS, C_DK)
    v = v.astype(f32).reshape(n, t, C_HEADS, C_DV)
    log_a = jax.nn.log_sigmoid((glr @ gate_w2 + gate_b).astype(f32)) / C_GATE_TAU
    log_a = log_a.reshape(n, t, C_HEADS, C_DK)
    o, s_new = _gla(q, k, v, log_a, gla_s.astype(f32))
    o = _rmsnorm(o.astype(h.dtype), norm_w) * jax.nn.silu(rg.reshape(n, t, C_HEADS, C_DV))
    return o.reshape(n, t, C_V) @ w_out, s_new.astype(gla_s.dtype)


def _conv_ffn(h, conv_buf, w_up, conv_w, w_down):
    u, new_buf = _causal_dwconv(h @ w_up, conv_buf, conv_w)
    gate, val = jnp.split(u, 2, axis=-1)
    return (jax.nn.gelu(gate) * val) @ w_down, new_buf


def _trunk(x, p, win_k, win_v, dn_conv, dn_state, gla_state, ffn_conv,
           norm_mix_pre, norm_mix_post, norm_ffn_pre, norm_ffn_post,
           w_in_even, w_out_even, dn_conv_w, dn_a_log, dn_dt_bias, dn_norm_w,
           w_in_odd, gla_gate_w2, gla_gate_b, gla_norm_w, w_out_odd,
           ffn_w_up, ffn_conv_w, ffn_w_down, ple_w_proj, ple_w_gate):
    r = x
    ks, vs, convs, dns, glas, ffns = [], [], [], [], [], []
    for i in range(DEPTH):
        j = i // 2
        h = _rmsnorm(r, norm_mix_pre[i])
        if i % 2 == 0:
            m, k_rows, v_rows, c_new, s_new = _mixer_even(
                h, win_k[j], win_v[j], dn_conv[j], dn_state[j], w_in_even[j], w_out_even[j],
                dn_conv_w[j], dn_a_log[j], dn_dt_bias[j], dn_norm_w[j])
            ks.append(k_rows)
            vs.append(v_rows)
            convs.append(c_new)
            dns.append(s_new)
        else:
            m, s_new = _mixer_odd(h, gla_state[j], w_in_odd[j], gla_gate_w2[j], gla_gate_b[j],
                                  gla_norm_w[j], w_out_odd[j])
            glas.append(s_new)
        r = r + _rmsnorm(m, norm_mix_post[i])
        h = _rmsnorm(r, norm_ffn_pre[i])
        f, f_new = _conv_ffn(h, ffn_conv[i], ffn_w_up[i], ffn_conv_w[i], ffn_w_down[i])
        ffns.append(f_new)
        r = r + _rmsnorm(f, norm_ffn_post[i])
        r = r + (p[i] @ ple_w_proj[i]) * jax.nn.sigmoid(r @ ple_w_gate[i])
    return r, jnp.stack(ks), jnp.stack(vs), jnp.stack(convs), jnp.stack(dns), jnp.stack(glas), jnp.stack(ffns)


def setup_inputs(seed: int = 0) -> dict:
    key = jax.random.key(seed)
    keys = jax.random.split(key, 40)
    counter = [0]

    def nxt():
        counter[0] += 1
        return keys[counter[0] - 1]

    def nrm(shape, scale=1.0):
        return scale * jax.random.normal(nxt(), shape, jnp.float32)

    def gain(shape):
        return 1.0 + 0.05 * jax.random.normal(nxt(), shape, jnp.float32)

    w_len = min(A_WMAX, PAST_LEN)
    x_prompt = nrm((BATCH, SEQ, D_MODEL))
    x_sample = nrm((DEC_BATCH, DEC_SEQ, D_MODEL))
    cache_win_k = nrm((N_EVEN, DEC_BATCH, w_len, A_HEADS, HEAD_DIM))
    cache_win_v = nrm((N_EVEN, DEC_BATCH, w_len, A_HEADS, HEAD_DIM))
    state_dn_conv = nrm((N_EVEN, DEC_BATCH, B_CONV - 1, B_QKV))
    state_dn = nrm((N_EVEN, DEC_BATCH, B_HEADS, B_DK, B_DV), 0.3)
    state_gla = nrm((N_ODD, DEC_BATCH, C_HEADS, C_DK, C_DV), 0.3)
    state_ffn_conv = nrm((DEPTH, DEC_BATCH, FFN_CONV - 1, 2 * D_FF))
    p_prompt = nrm((DEPTH, BATCH, SEQ, PLE_DIM))
    p_sample = nrm((DEPTH, DEC_BATCH, DEC_SEQ, PLE_DIM))
    norm_mix_pre = gain((DEPTH, D_MODEL))
    norm_mix_post = gain((DEPTH, D_MODEL))
    norm_ffn_pre = gain((DEPTH, D_MODEL))
    norm_ffn_post = gain((DEPTH, D_MODEL))
    w_in_even = nrm((N_EVEN, D_MODEL, N_IN_EVEN), D_MODEL ** -0.5)
    w_out_even = nrm((N_EVEN, N_MIX_EVEN, D_MODEL), N_MIX_EVEN ** -0.5)
    dn_conv_w = nrm((N_EVEN, B_CONV, B_QKV), B_CONV ** -0.5)
    dn_a_log = jnp.log(jax.random.uniform(nxt(), (N_EVEN, B_HEADS), jnp.float32, 1.0, 16.0))
    dt = jnp.exp(jax.random.uniform(nxt(), (N_EVEN, B_HEADS), jnp.float32, math.log(1e-3), math.log(1e-1)))
    dn_dt_bias = dt + jnp.log(-jnp.expm1(-dt))
    dn_norm_w = gain((N_EVEN, B_DV))
    w_in_odd = nrm((N_ODD, D_MODEL, N_IN_ODD), D_MODEL ** -0.5)
    gla_gate_w2 = nrm((N_ODD, C_GATE_RANK, C_QK), C_GATE_RANK ** -0.5)
    gla_gate_b = nrm((N_ODD, C_QK), 0.1)
    gla_norm_w = gain((N_ODD, C_DV))
    w_out_odd = nrm((N_ODD, C_V, D_MODEL), C_V ** -0.5)
    ffn_w_up = nrm((DEPTH, D_MODEL, 2 * D_FF), D_MODEL ** -0.5)
    ffn_conv_w = nrm((DEPTH, FFN_CONV, 2 * D_FF), FFN_CONV ** -0.5)
    ffn_w_down = nrm((DEPTH, D_FF, D_MODEL), D_FF ** -0.5)
    ple_w_proj = nrm((DEPTH, PLE_DIM, D_MODEL), PLE_DIM ** -0.5)
    ple_w_gate = nrm((DEPTH, D_MODEL, D_MODEL), D_MODEL ** -0.5)
    return {
        'x_prompt': x_prompt, 'x_sample': x_sample,
        'cache_win_k': cache_win_k, 'cache_win_v': cache_win_v,
        'state_dn_conv': state_dn_conv, 'state_dn': state_dn, 'state_gla': state_gla,
        'state_ffn_conv': state_ffn_conv,
        'p_prompt': p_prompt, 'p_sample': p_sample,
        'norm_mix_pre': norm_mix_pre, 'norm_mix_post': norm_mix_post,
        'norm_ffn_pre': norm_ffn_pre, 'norm_ffn_post': norm_ffn_post,
        'w_in_even': w_in_even, 'w_out_even': w_out_even, 'dn_conv_w': dn_conv_w,
        'dn_a_log': dn_a_log, 'dn_dt_bias': dn_dt_bias, 'dn_norm_w': dn_norm_w,
        'w_in_odd': w_in_odd, 'gla_gate_w2': gla_gate_w2, 'gla_gate_b': gla_gate_b,
        'gla_norm_w': gla_norm_w, 'w_out_odd': w_out_odd,
        'ffn_w_up': ffn_w_up, 'ffn_conv_w': ffn_conv_w, 'ffn_w_down': ffn_w_down,
        'ple_w_proj': ple_w_proj, 'ple_w_gate': ple_w_gate,
    }


def reference(x_prompt, x_sample, cache_win_k, cache_win_v, state_dn_conv, state_dn, state_gla,
              state_ffn_conv, p_prompt, p_sample, norm_mix_pre, norm_mix_post, norm_ffn_pre,
              norm_ffn_post, w_in_even, w_out_even, dn_conv_w, dn_a_log, dn_dt_bias, dn_norm_w,
              w_in_odd, gla_gate_w2, gla_gate_b, gla_norm_w, w_out_odd, ffn_w_up, ffn_conv_w,
              ffn_w_down, ple_w_proj, ple_w_gate):
    weights = (norm_mix_pre, norm_mix_post, norm_ffn_pre, norm_ffn_post,
               w_in_even, w_out_even, dn_conv_w, dn_a_log, dn_dt_bias, dn_norm_w,
               w_in_odd, gla_gate_w2, gla_gate_b, gla_norm_w, w_out_odd,
               ffn_w_up, ffn_conv_w, ffn_w_down, ple_w_proj, ple_w_gate)
    nb, seq = x_prompt.shape[0], x_prompt.shape[1]
    dt = x_prompt.dtype
    empty_win = jnp.zeros((N_EVEN, nb, 0, A_HEADS, HEAD_DIM), dt)
    y_prompt, k_p, v_p, dc_p, ds_p, gs_p, fc_p = _trunk(
        x_prompt, p_prompt, empty_win, empty_win,
        jnp.zeros((N_EVEN, nb, B_CONV - 1, B_QKV), dt),
        jnp.zeros((N_EVEN, nb, B_HEADS, B_DK, B_DV), jnp.float32),
        jnp.zeros((N_ODD, nb, C_HEADS, C_DK, C_DV), jnp.float32),
        jnp.zeros((DEPTH, nb, FFN_CONV - 1, 2 * D_FF), dt),
        *weights)
    n_keep = min(A_WMAX, seq)
    k_p = k_p[:, :, seq - n_keep:]
    v_p = v_p[:, :, seq - n_keep:]
    y_sample, k_s, v_s, dc_s, ds_s, gs_s, fc_s = _trunk(
        x_sample, p_sample, cache_win_k, cache_win_v, state_dn_conv, state_dn, state_gla,
        state_ffn_conv, *weights)
    return (y_prompt, y_sample, k_p, v_p, dc_p, ds_p, gs_p, fc_p, k_s, v_s, dc_s, ds_s, gs_s, fc_s)
```

```python
import functools
import math

import jax
import jax.numpy as jnp
from jax import lax
from jax.experimental import pallas as pl
from jax.experimental.pallas import tpu as pltpu

F32 = jnp.float32
BF16 = jnp.bfloat16
EPS = 1e-6
NEG = -1e30
HIGHEST = lax.Precision.HIGHEST

LANES = 128
V7X_VMEM_LIMIT_BYTES = 56 * 2**20

HEAD_DIM = 128
A_HEADS = 8
A_KEYS = 128
A_DILATIONS = (1, 4, 16)
A_WMAX = 2048
B_HEADS = 8
B_DK = 128
B_CONV = 4
C_HEADS = 4
C_GATE_RANK = 16
C_GATE_TAU = 16.0
FFN_CONV = 3
DN_CHUNK = 64
GLA_CHUNK = 64
GLA_SUB = 16
PROJ_TN = 1280


def _cparams(*sem):
    return pltpu.CompilerParams(dimension_semantics=sem, vmem_limit_bytes=V7X_VMEM_LIMIT_BYTES)


def _dot(a, b):
    return jnp.dot(a.astype(BF16), b.astype(BF16), preferred_element_type=F32)


def _dot_nt(a, b):
    return lax.dot_general(a.astype(BF16), b.astype(BF16), (((1,), (1,)), ((), ())),
                           preferred_element_type=F32)


def _dot_tn(a, b):
    return lax.dot_general(a.astype(BF16), b.astype(BF16), (((0,), (0,)), ((), ())),
                           preferred_element_type=F32)


def _dot_f32(a, b):
    return jnp.dot(a, b, precision=HIGHEST, preferred_element_type=F32)


def _dot_tn_f32(a, b):
    return lax.dot_general(a, b, (((0,), (0,)), ((), ())), precision=HIGHEST,
                           preferred_element_type=F32)


def _rms(x, g):
    return x * lax.rsqrt(jnp.mean(x * x, axis=-1, keepdims=True) + EPS) * g


def _sigmoid(x):
    return 1.0 / (1.0 + jnp.exp(-x))


def _softplus(x):
    return jnp.maximum(x, 0.0) + jnp.log(1.0 + jnp.exp(-jnp.abs(x)))


def _gelu_tanh(x):
    return 0.5 * x * (1.0 + jnp.tanh(math.sqrt(2.0 / math.pi) * (x + 0.044715 * (x * x * x))))


def _rms_matmul_kernel(x_ref, g_ref, w_ref, o_ref, h_ref):
    @pl.when(pl.program_id(1) == 0)
    def _():
        h_ref[...] = _rms(x_ref[...], g_ref[...]).astype(BF16)

    o_ref[...] = jnp.dot(h_ref[...], w_ref[...], preferred_element_type=F32)


def _rms_matmul(x, g, w, tm, tn):
    m, k = x.shape
    n = w.shape[1]
    return pl.pallas_call(
        _rms_matmul_kernel,
        grid=(m // tm, n // tn),
        in_specs=[pl.BlockSpec((tm, k), lambda i, j: (i, 0)),
                  pl.BlockSpec((1, k), lambda i, j: (0, 0)),
                  pl.BlockSpec((k, tn), lambda i, j: (0, j))],
        out_specs=pl.BlockSpec((tm, tn), lambda i, j: (i, j)),
        out_shape=jax.ShapeDtypeStruct((m, n), F32),
        scratch_shapes=[pltpu.VMEM((tm, k), BF16)],
        compiler_params=_cparams("parallel", "arbitrary"),
        name="rms_matmul",
    )(x, g.reshape(1, k), w)


ATTN_SB = A_WMAX


def _attn_prompt_kernel(q_ref, kc_ref, kp_ref, vc_ref, vp_ref, o_ref, o_s, l_s, *, scale):
    has_prev = pl.program_id(2) > 0
    row = lax.broadcasted_iota(jnp.int32, (A_KEYS, 2 * A_KEYS), 0)
    col = lax.broadcasted_iota(jnp.int32, (A_KEYS, 2 * A_KEYS), 1)
    cur_ok = (col >= A_KEYS) & (col - A_KEYS <= row)
    prev_ok = (col < A_KEYS) & (col >= row)

    def tile(ref, start, d):
        if d == 1:
            return ref[0, pl.ds(start, A_KEYS), :]
        return ref[0, pl.ds(start, A_KEYS, stride=d), :]

    for g, d in enumerate(A_DILATIONS):
        span = A_KEYS * d
        for r in range(d):
            for j in range(ATTN_SB // span):
                start = r + span * j
                q = tile(q_ref, start, d) * scale
                if j > 0:
                    kp, vp = tile(kc_ref, start - span, d), tile(vc_ref, start - span, d)
                    ok = cur_ok | prev_ok
                else:
                    kp, vp = tile(kp_ref, ATTN_SB - span + r, d), tile(vp_ref, ATTN_SB - span + r, d)
                    ok = cur_ok | (prev_ok & has_prev)
                s = _dot_nt(q, jnp.concatenate([kp, tile(kc_ref, start, d)], axis=0))
                s = jnp.where(ok, s, NEG)
                m = jnp.max(s, axis=-1, keepdims=True)
                p = jnp.exp(s - m)
                l = jnp.sum(p, axis=-1, keepdims=True)
                o = _dot(p, jnp.concatenate([vp, tile(vc_ref, start, d)], axis=0)) / l
                lse = jnp.broadcast_to(m + jnp.log(l), (A_KEYS, HEAD_DIM))
                if d == 1:
                    o_s[g, pl.ds(start, A_KEYS), :] = o
                    l_s[g, pl.ds(start, A_KEYS), :] = lse
                else:
                    o_s[g, pl.ds(start, A_KEYS, stride=d), :] = o
                    l_s[g, pl.ds(start, A_KEYS, stride=d), :] = lse
    la, lb, lc = l_s[0], l_s[1], l_s[2]
    m = jnp.maximum(jnp.maximum(la, lb), lc)
    wa, wb, wc = jnp.exp(la - m), jnp.exp(lb - m), jnp.exp(lc - m)
    o_ref[0] = (wa * o_s[0] + wb * o_s[1] + wc * o_s[2]) / (wa + wb + wc)


def _attn_prompt(proj, n, t):
    assert t % ATTN_SB == 0 and len(A_DILATIONS) == 3
    cur = lambda off: (lambda b, h, i: (b, i, off + h))
    prev = lambda off: (lambda b, h, i: (b, jnp.maximum(i - 1, 0), off + h))
    blk = (1, ATTN_SB, HEAD_DIM)
    return pl.pallas_call(
        functools.partial(_attn_prompt_kernel, scale=HEAD_DIM ** -0.5),
        grid=(n, A_HEADS, t // ATTN_SB),
        in_specs=[pl.BlockSpec(blk, cur(0)),
                  pl.BlockSpec(blk, cur(A_HEADS)), pl.BlockSpec(blk, prev(A_HEADS)),
                  pl.BlockSpec(blk, cur(2 * A_HEADS)), pl.BlockSpec(blk, prev(2 * A_HEADS))],
        out_specs=pl.BlockSpec(blk, lambda b, h, i: (b, i, h)),
        out_shape=jax.ShapeDtypeStruct((n, t, A_HEADS * HEAD_DIM), F32),
        scratch_shapes=[pltpu.VMEM((3, ATTN_SB, HEAD_DIM), F32), pltpu.VMEM((3, ATTN_SB, HEAD_DIM), F32)],
        compiler_params=_cparams("parallel", "parallel", "arbitrary"), name="attn_prompt",
    )(proj, proj, proj, proj, proj)


def _attn_sample_kernel(x_ref, k0_ref, k1_ref, k2_ref, v0_ref, v1_ref, v2_ref, o_ref, *, scale):
    tn = 4
    nq = tn * A_HEADS
    nk = A_KEYS * A_HEADS
    x = x_ref[0]
    q = (x[:, 0:A_HEADS] * scale).reshape(nq, HEAD_DIM)
    knew = x[:, A_HEADS:2 * A_HEADS].reshape(nq, HEAD_DIM)
    vnew = x[:, 2 * A_HEADS:3 * A_HEADS].reshape(nq, HEAD_DIM)

    def grid_masks(cols):
        qrow = lax.broadcasted_iota(jnp.int32, (nq, cols), 0)
        kcol = lax.broadcasted_iota(jnp.int32, (nq, cols), 1)
        return (qrow % A_HEADS) == (kcol % A_HEADS), qrow // A_HEADS, kcol // A_HEADS

    same_head, qt, key = grid_masks(nk)
    flat = lambda a: a.reshape(nk, HEAD_DIM)
    segs = [(flat(k0_ref[0]), flat(v0_ref[0]), jnp.where(same_head & (key >= qt), 1.0, 0.0))]
    for kr, vr in ((k1_ref, v1_ref), (k2_ref, v2_ref)):
        for t in range(tn):
            segs.append((flat(kr[0, :, t]), flat(vr[0, :, t]), jnp.where(same_head & (qt == t), 1.0, 0.0)))
    same_head, qt, kt = grid_masks(nq)
    segs.append((knew, vnew, jnp.where(same_head & (kt == qt), float(len(A_DILATIONS)),
                                       jnp.where(same_head & (kt < qt), 1.0, 0.0))))

    scores = [_dot_nt(q, k) for k, _, _ in segs]
    m = None
    for s, (_, _, mult) in zip(scores, segs):
        ms = jnp.max(jnp.where(mult > 0.0, s, NEG), axis=-1, keepdims=True)
        m = ms if m is None else jnp.maximum(m, ms)
    probs = [jnp.where(mult > 0.0, jnp.exp(s - m), 0.0) * mult for s, (_, _, mult) in zip(scores, segs)]
    l = sum(jnp.sum(p, axis=-1, keepdims=True) for p in probs)
    acc = sum(_dot(p, v) for p, (_, v, _) in zip(probs, segs))
    o_ref[0] = (acc / l).reshape(tn, A_HEADS, HEAD_DIM)


def _attn_sample(x4, cache_k, cache_v):
    n, tn = x4.shape[0], x4.shape[1]
    w = cache_k.shape[1]
    assert w == A_WMAX and tn == 4
    views = []
    for cache in (cache_k, cache_v):
        views += [cache, cache.reshape(n, w // 4, 4, A_HEADS, HEAD_DIM),
                  cache.reshape(n, w // 16, 16, A_HEADS, HEAD_DIM)]
    k0, k1, k2, v0, v1, v2 = views
    s0 = pl.BlockSpec((1, A_KEYS, A_HEADS, HEAD_DIM), lambda b: (b, w // A_KEYS - 1, 0, 0))
    s1 = pl.BlockSpec((1, A_KEYS, 4, A_HEADS, HEAD_DIM), lambda b: (b, w // 4 // A_KEYS - 1, 0, 0, 0))
    s2 = pl.BlockSpec((1, A_KEYS, 4, A_HEADS, HEAD_DIM), lambda b: (b, 0, 0, 0, 0))
    return pl.pallas_call(
        functools.partial(_attn_sample_kernel, scale=HEAD_DIM ** -0.5),
        grid=(n,),
        in_specs=[pl.BlockSpec((1, tn, 3 * A_HEADS, HEAD_DIM), lambda b: (b, 0, 0, 0)), s0, s1, s2, s0, s1, s2],
        out_specs=pl.BlockSpec((1, tn, A_HEADS, HEAD_DIM), lambda b: (b, 0, 0, 0)),
        out_shape=jax.ShapeDtypeStruct((n, tn, A_HEADS, HEAD_DIM), F32),
        compiler_params=_cparams("parallel"), name="attn_sample",
    )(x4, k0, k1, k2, v0, v1, v2)


def _unit_lower_inverse_minus_identity(low):
    c = low.shape[0]
    base = min(16, c)
    row = lax.broadcasted_iota(jnp.int32, (c, c), 0)
    col = lax.broadcasted_iota(jnp.int32, (c, c), 1)
    nil = jnp.where((row // base) == (col // base), -low, 0.0)
    q = nil
    pw = nil
    for _ in range(int(math.log2(base)) - 1):
        pw = _dot(pw, pw)
        q = q + pw + _dot(q, pw)
    b = base
    while b < c:
        sib = ((row // (2 * b)) == (col // (2 * b))) & ((row // b) != (col // b))
        off = jnp.where(sib, low, 0.0)
        t = off + _dot(q, off)
        q = q - t - _dot(t, q)
        b *= 2
    return q


def _dn_prompt_kernel(x_ref, ba_ref, cw_ref, adt_ref, cinit_ref, sinit_ref, o_ref, sout_ref,
                      xbuf, s_ref, *, c):
    ci = pl.program_id(1)
    halo = 8
    hk = B_HEADS * B_DK

    @pl.when(ci == 0)
    def _():
        xbuf[0:halo, :] = cinit_ref[0]
        s_ref[...] = sinit_ref[0]

    xbuf[halo:halo + c, :] = x_ref[0]
    cw = cw_ref[...]
    y = cw[0:1, :] * xbuf[pl.ds(halo - B_CONV + 1, c), :]
    for i in range(1, B_CONV):
        y = y + cw[i:i + 1, :] * xbuf[pl.ds(halo - B_CONV + 1 + i, c), :]
    xbuf[0:halo, :] = xbuf[c:c + halo, :]
    cq = y * _sigmoid(y)

    ba = ba_ref[0]
    beta_all = _sigmoid(ba)
    g_all = -jnp.exp(adt_ref[0:1, :]) * _softplus(ba + adt_ref[1:2, :])
    row = lax.broadcasted_iota(jnp.int32, (c, c), 0)
    col = lax.broadcasted_iota(jnp.int32, (c, c), 1)
    gcum_col = _dot_f32((row >= col).astype(F32), g_all)
    gcum_row = _dot_tn_f32(g_all, (row <= col).astype(F32))

    outs = []
    for h in range(B_HEADS):
        gc = gcum_col[:, B_HEADS + h:B_HEADS + h + 1]
        gr = gcum_row[B_HEADS + h:B_HEADS + h + 1, :]
        decay = jnp.exp(jnp.where(row >= col, gc - gr, NEG))
        qh = cq[:, h * B_DK:(h + 1) * B_DK]
        kh = cq[:, hk + h * B_DK:hk + (h + 1) * B_DK]
        vh = cq[:, 2 * hk + h * B_DK:2 * hk + (h + 1) * B_DK]
        qh = qh * lax.rsqrt(jnp.sum(qh * qh, axis=-1, keepdims=True) + EPS) * (B_DK ** -0.5)
        kh = kh * lax.rsqrt(jnp.sum(kh * kh, axis=-1, keepdims=True) + EPS)
        bh = beta_all[:, h:h + 1]
        kb = kh * bh
        low = jnp.where(row > col, _dot_nt(kb, kh) * decay, 0.0)
        qinv = _unit_lower_inverse_minus_identity(low)
        eg = jnp.exp(gc)
        rhs = jnp.concatenate([vh * bh, kb * eg], axis=1)
        sol = rhs + _dot(qinv, rhs)
        u = sol[:, 0:B_DK]
        w = sol[:, B_DK:2 * B_DK]
        attn = _dot_nt(qh, kh) * decay
        glast = gc[c - 1:c, :]
        kd = kh * jnp.exp(glast - gc)
        s = s_ref[h]
        v_new = u - _dot(w, s)
        outs.append(_dot(qh * eg, s) + _dot(attn, v_new))
        s_ref[h] = s * jnp.exp(glast) + _dot_tn(kd, v_new)
    o_ref[0] = jnp.concatenate(outs, axis=1)

    @pl.when(ci == pl.num_programs(1) - 1)
    def _():
        sout_ref[0] = s_ref[...]


def _dn_prompt(proj, conv_w, adt, conv_init, s_init):
    n, t, _ = proj.shape
    c = math.gcd(t, DN_CHUNK)
    width = 3 * B_HEADS * B_DK
    return pl.pallas_call(
        functools.partial(_dn_prompt_kernel, c=c),
        grid=(n, t // c),
        in_specs=[pl.BlockSpec((1, c, width), lambda b, i: (b, i, 1)),
                  pl.BlockSpec((1, c, LANES), lambda b, i: (b, i, (2 * width + B_HEADS * B_DK) // LANES)),
                  pl.BlockSpec((B_CONV, width), lambda b, i: (0, 0)),
                  pl.BlockSpec((2, LANES), lambda b, i: (0, 0)),
                  pl.BlockSpec((1, 8, width), lambda b, i: (b, 0, 0)),
                  pl.BlockSpec((1, B_HEADS, B_DK, B_DK), lambda b, i: (b, 0, 0, 0))],
        out_specs=[pl.BlockSpec((1, c, B_HEADS * B_DK), lambda b, i: (b, i, 0)),
                   pl.BlockSpec((1, B_HEADS, B_DK, B_DK), lambda b, i: (b, 0, 0, 0))],
        out_shape=[jax.ShapeDtypeStruct((n, t, B_HEADS * B_DK), F32),
                   jax.ShapeDtypeStruct((n, B_HEADS, B_DK, B_DK), F32)],
        scratch_shapes=[pltpu.VMEM((8 + c, width), F32), pltpu.VMEM((B_HEADS, B_DK, B_DK), F32)],
        compiler_params=_cparams("parallel", "arbitrary"), name="deltanet_prompt",
    )(proj, proj, conv_w, adt, conv_init, s_init)


def _dn_sample_kernel(x_ref, bat_ref, cw_ref, adt_ref, cs_ref, sinit_ref, o_ref, sout_ref, s_ref):
    tn = 4
    hk = B_HEADS * B_DK
    xp = jnp.concatenate([cs_ref[0], x_ref[0]], axis=0)
    cw = cw_ref[...]
    y = cw[0] * xp[0:tn]
    for i in range(1, B_CONV):
        y = y + cw[i] * xp[i:i + tn]
    cq = y * _sigmoid(y)
    q = cq[:, 0:B_HEADS]
    k = cq[:, B_HEADS:2 * B_HEADS]
    v = cq[:, 2 * B_HEADS:3 * B_HEADS]
    q = q * lax.rsqrt(jnp.sum(q * q, axis=-1, keepdims=True) + EPS) * (B_DK ** -0.5)
    k = k * lax.rsqrt(jnp.sum(k * k, axis=-1, keepdims=True) + EPS)
    bat = bat_ref[0]
    beta = _sigmoid(bat[0:B_HEADS, :])
    a = jnp.exp(-jnp.exp(adt_ref[:, 0:1]) * _softplus(bat[B_HEADS:2 * B_HEADS, :] + adt_ref[:, 1:2]))
    lane_head = lax.broadcasted_iota(jnp.int32, (B_HEADS, hk), 1) // B_DK
    head_mask = lane_head == lax.broadcasted_iota(jnp.int32, (B_HEADS, hk), 0)

    def block_diag(x):
        return jnp.where(head_mask, jnp.concatenate([x] * B_HEADS, axis=1), 0.0)

    s_ref[...] = sinit_ref[0]
    for t in range(tn):
        kbd = block_diag(k[t])
        at = a[:, t:t + 1]
        ks = _dot(kbd, s_ref[...])
        w = beta[:, t:t + 1] * (v[t] - at * ks)
        upd = _dot_tn(kbd, w)
        for h in range(B_HEADS):
            rows = slice(h * B_DK, (h + 1) * B_DK)
            s_ref[rows, :] = s_ref[rows, :] * at[h:h + 1, :] + upd[rows, :]
        o_ref[0, t] = _dot(block_diag(q[t]), s_ref[...])
    sout_ref[0] = s_ref[...]


def _dn_sample(x4, bat, conv_w, adt_col, conv_state, s_init):
    n = x4.shape[0]
    rows = 3 * B_HEADS
    hk = B_HEADS * B_DK
    return pl.pallas_call(
        _dn_sample_kernel,
        grid=(n,),
        in_specs=[pl.BlockSpec((1, 4, rows, B_DK), lambda b: (b, 0, 1, 0)),
                  pl.BlockSpec((1, LANES, 4), lambda b: (b, 0, 0)),
                  pl.BlockSpec((B_CONV, rows, B_DK), lambda b: (0, 0, 0)),
                  pl.BlockSpec((B_HEADS, 2), lambda b: (0, 0)),
                  pl.BlockSpec((1, B_CONV - 1, rows, B_DK), lambda b: (b, 0, 0, 0)),
                  pl.BlockSpec((1, hk, B_DK), lambda b: (b, 0, 0))],
        out_specs=[pl.BlockSpec((1, 4, B_HEADS, B_DK), lambda b: (b, 0, 0, 0)),
                   pl.BlockSpec((1, hk, B_DK), lambda b: (b, 0, 0))],
        out_shape=[jax.ShapeDtypeStruct((n, 4, B_HEADS, B_DK), F32),
                   jax.ShapeDtypeStruct((n, hk, B_DK), F32)],
        scratch_shapes=[pltpu.VMEM((hk, B_DK), F32)],
        compiler_params=_cparams("parallel"), name="deltanet_sample",
    )(x4, bat, conv_w, adt_col, conv_state, s_init)


def _gla_kernel(q_ref, k_ref, v_ref, glr_ref, w2_ref, gb_ref, sinit_ref, o_ref, sout_ref,
                s_ref, qs, ks, gs, vs, od, *, c_real, c, dk, dv):
    ci = pl.program_id(1)
    sub = min(GLA_SUB, c)

    @pl.when(ci == 0)
    def _():
        s_ref[...] = sinit_ref[0]

    def load(ref):
        x = ref[0]
        if c_real < c:
            x = jnp.concatenate([x, jnp.zeros((c - c_real, x.shape[1]), F32)], axis=0)
        return x

    qa, ka, va, glr = load(q_ref), load(k_ref), load(v_ref), load(glr_ref)
    row = lax.broadcasted_iota(jnp.int32, (c, c), 0)
    col = lax.broadcasted_iota(jnp.int32, (c, c), 1)
    tri = (row >= col).astype(F32)
    row_valid = lax.broadcasted_iota(jnp.int32, (c, 1), 0) < c_real
    ones = jnp.ones((c, LANES), F32)
    outs = []
    for h in range(C_HEADS):
        pre = _dot(glr, w2_ref[:, h * dk:(h + 1) * dk]) + gb_ref[:, h * dk:(h + 1) * dk]
        la = jnp.where(row_valid, (jnp.minimum(pre, 0.0) - jnp.log(1.0 + jnp.exp(-jnp.abs(pre)))) / C_GATE_TAU, 0.0)
        gc = _dot_f32(tri, la)
        q = qa[:, h * dk:(h + 1) * dk] * (dk ** -0.5)
        k = ka[:, h * dk:(h + 1) * dk]
        v = va[:, h * dv:(h + 1) * dv]
        qs[...] = q
        ks[...] = k
        gs[...] = gc
        vs[...] = v

        def diag_row(i, carry):
            b0 = pl.multiple_of((i // sub) * sub, sub)
            qi = qs[pl.ds(i, 1), :]
            gi = gs[pl.ds(i, 1), :]
            kb = ks[pl.ds(b0, sub), :]
            gb = gs[pl.ds(b0, sub), :]
            vb = vs[pl.ds(b0, sub), :]
            jj = b0 + lax.broadcasted_iota(jnp.int32, (sub, 1), 0)
            e = jnp.exp(jnp.minimum(gi - gb, 0.0))
            a = jnp.sum(jnp.where(jj <= i, kb * qi * e, 0.0), axis=-1, keepdims=True)
            od[pl.ds(i, 1), :] = jnp.sum(a * vb, axis=0, keepdims=True)
            return carry

        lax.fori_loop(0, c, diag_row, 0)
        o = od[...]
        if c > sub:
            blocks = [jnp.zeros((sub, dv), F32)]
            for b in range(1, c // sub):
                lo = b * sub
                ref_pt = gc[lo:lo + 1, :]
                qsc = q[lo:lo + sub, :] * jnp.exp(gc[lo:lo + sub, :] - ref_pt)
                ksc = k[0:lo, :] * jnp.exp(ref_pt - gc[0:lo, :])
                blocks.append(_dot(_dot_nt(qsc, ksc), v[0:lo, :]))
            o = o + jnp.concatenate(blocks, axis=0)
        s = s_ref[h]
        o = o + _dot(q * jnp.exp(gc), s)
        glast = gc[c - 1:c, :]
        kd = k * jnp.exp(glast - gc)
        gl_col = jnp.exp(_dot_tn_f32(la, ones))[:, 0:1]
        s_ref[h] = s * gl_col + _dot_tn(kd, v)
        outs.append(o[0:c_real, :])
    o_ref[0] = jnp.concatenate(outs, axis=1)

    @pl.when(ci == pl.num_programs(1) - 1)
    def _():
        sout_ref[0] = s_ref[...]


def _gla(proj, w2, gb, s_init):
    n, t, _ = proj.shape
    dk, dv = s_init.shape[2], s_init.shape[3]
    c_real = math.gcd(t, GLA_CHUNK)
    c = max(c_real, 8)
    hk, hv = C_HEADS * dk, C_HEADS * dv
    return pl.pallas_call(
        functools.partial(_gla_kernel, c_real=c_real, c=c, dk=dk, dv=dv),
        grid=(n, t // c_real),
        in_specs=[pl.BlockSpec((1, c_real, hk), lambda b, i: (b, i, 0)),
                  pl.BlockSpec((1, c_real, hk), lambda b, i: (b, i, 1)),
                  pl.BlockSpec((1, c_real, hv), lambda b, i: (b, i, 2 * hk // hv)),
                  pl.BlockSpec((1, c_real, LANES), lambda b, i: (b, i, (2 * hk + 2 * hv) // LANES)),
                  pl.BlockSpec((LANES, hk), lambda b, i: (0, 0)),
                  pl.BlockSpec((1, hk), lambda b, i: (0, 0)),
                  pl.BlockSpec((1, C_HEADS, dk, dv), lambda b, i: (b, 0, 0, 0))],
        out_specs=[pl.BlockSpec((1, c_real, hv), lambda b, i: (b, i, 0)),
                   pl.BlockSpec((1, C_HEADS, dk, dv), lambda b, i: (b, 0, 0, 0))],
        out_shape=[jax.ShapeDtypeStruct((n, t, hv), F32),
                   jax.ShapeDtypeStruct((n, C_HEADS, dk, dv), F32)],
        scratch_shapes=[pltpu.VMEM((C_HEADS, dk, dv), F32), pltpu.VMEM((c, dk), F32),
                        pltpu.VMEM((c, dk), F32), pltpu.VMEM((c, dk), F32),
                        pltpu.VMEM((c, dv), F32), pltpu.VMEM((c, dv), F32)],
        compiler_params=_cparams("parallel", "arbitrary"), name="gla",
    )(proj, proj, proj, proj, w2, gb, s_init)


def _gated_headnorm(o, z, nw, heads, width):
    parts = []
    for h in range(heads):
        zz = z[:, h * width:(h + 1) * width]
        parts.append(_rms(o[:, h * width:(h + 1) * width], nw) * (zz * _sigmoid(zz)))
    return jnp.concatenate(parts, axis=1)


def _mix_even_kernel(oa_ref, ob_ref, z_ref, nw_ref, w_ref, r_ref, g_ref, out_ref):
    ka = A_HEADS * HEAD_DIM
    obn = _gated_headnorm(ob_ref[...], z_ref[...], nw_ref[...], B_HEADS, B_DK)
    m = _dot(oa_ref[...], w_ref[0:ka, :]) + _dot(obn, w_ref[ka:, :])
    out_ref[...] = r_ref[...] + _rms(m, g_ref[...])


def _mix_even(o_a, o_b, proj, nw, w_out, r, g_post, tm):
    m, d = r.shape
    ka = o_a.shape[1]
    kb = o_b.shape[1]
    z_block = (2 * 3 * A_HEADS * HEAD_DIM) // kb
    return pl.pallas_call(
        _mix_even_kernel, grid=(m // tm,),
        in_specs=[pl.BlockSpec((tm, ka), lambda i: (i, 0)),
                  pl.BlockSpec((tm, kb), lambda i: (i, 0)),
                  pl.BlockSpec((tm, kb), lambda i: (i, z_block)),
                  pl.BlockSpec((1, B_DK), lambda i: (0, 0)),
                  pl.BlockSpec((ka + kb, d), lambda i: (0, 0)),
                  pl.BlockSpec((tm, d), lambda i: (i, 0)),
                  pl.BlockSpec((1, d), lambda i: (0, 0))],
        out_specs=pl.BlockSpec((tm, d), lambda i: (i, 0)),
        out_shape=jax.ShapeDtypeStruct((m, d), F32),
        compiler_params=_cparams("parallel"), name="mix_even",
    )(o_a, o_b, proj, nw.reshape(1, -1), w_out, r, g_post.reshape(1, d))


def _mix_odd_kernel(o_ref, z_ref, nw_ref, w_ref, r_ref, g_ref, out_ref, *, dv):
    on = _gated_headnorm(o_ref[...], z_ref[...], nw_ref[...], C_HEADS, dv)
    out_ref[...] = r_ref[...] + _rms(_dot(on, w_ref[...]), g_ref[...])


def _mix_odd(o, proj, nw, w_out, r, g_post, tm):
    m, d = r.shape
    kv = o.shape[1]
    return pl.pallas_call(
        functools.partial(_mix_odd_kernel, dv=kv // C_HEADS), grid=(m // tm,),
        in_specs=[pl.BlockSpec((tm, kv), lambda i: (i, 0)),
                  pl.BlockSpec((tm, kv), lambda i: (i, 2)),
                  pl.BlockSpec((1, kv // C_HEADS), lambda i: (0, 0)),
                  pl.BlockSpec((kv, d), lambda i: (0, 0)),
                  pl.BlockSpec((tm, d), lambda i: (i, 0)),
                  pl.BlockSpec((1, d), lambda i: (0, 0))],
        out_specs=pl.BlockSpec((tm, d), lambda i: (i, 0)),
        out_shape=jax.ShapeDtypeStruct((m, d), F32),
        compiler_params=_cparams("parallel"), name="mix_odd",
    )(o, proj, nw.reshape(1, -1), w_out, r, g_post.reshape(1, d))


def _ffn_kernel(x_ref, gpre_ref, wg_ref, wv_ref, cwg_ref, cwv_ref, wd_ref, gpost_ref, ig_ref, iv_ref,
                o_ref, tg_ref, tv_ref, h_ref, ug, uv, *carry, tm, halo, shift):
    i = pl.program_id(1)
    j = pl.program_id(2)

    @pl.when(j == 0)
    def _():
        h_ref[...] = _rms(x_ref[0], gpre_ref[...]).astype(BF16)
        o_ref[0] = jnp.zeros_like(o_ref[0])

    if carry:
        cg, cv = carry

        @pl.when(i == 0)
        def _():
            ug[0:halo, :] = ig_ref[0]
            uv[0:halo, :] = iv_ref[0]

        @pl.when(i > 0)
        def _():
            ug[0:halo, :] = cg[j]
            uv[0:halo, :] = cv[j]
    else:
        ug[0:halo, :] = ig_ref[0]
        uv[0:halo, :] = iv_ref[0]

    ug[halo:halo + tm, :] = jnp.dot(h_ref[...], wg_ref[...], preferred_element_type=F32)
    uv[halo:halo + tm, :] = jnp.dot(h_ref[...], wv_ref[...], preferred_element_type=F32)

    def conv(u, cw):
        y = cw[0:1, :] * u[pl.ds(halo - (FFN_CONV - 1) * shift, tm), :]
        for tap in range(1, FFN_CONV):
            y = y + cw[tap:tap + 1, :] * u[pl.ds(halo - (FFN_CONV - 1 - tap) * shift, tm), :]
        return y

    act = _gelu_tanh(conv(ug, cwg_ref[...])) * conv(uv, cwv_ref[...])
    tail_g = ug[tm:tm + halo, :]
    tail_v = uv[tm:tm + halo, :]
    tg_ref[0, 0] = tail_g
    tv_ref[0, 0] = tail_v
    if carry:
        cg[j] = tail_g
        cv[j] = tail_v
    o_ref[0] += _dot(act, wd_ref[...])

    @pl.when(j == pl.num_programs(2) - 1)
    def _():
        o_ref[0] = x_ref[0] + _rms(o_ref[0], gpost_ref[...])


def _ffn(x, g_pre, w_up, conv_w, w_down, g_post, init, tm, tf, shift):
    n, t, d = x.shape
    dff = w_down.shape[0]
    halo = init.shape[1]
    ni, nj = t // tm, dff // tf
    scratch = [pltpu.VMEM((tm, d), BF16), pltpu.VMEM((halo + tm, tf), F32), pltpu.VMEM((halo + tm, tf), F32)]
    if ni > 1:
        scratch += [pltpu.VMEM((nj, halo, tf), F32), pltpu.VMEM((nj, halo, tf), F32)]
    out, tail_g, tail_v = pl.pallas_call(
        functools.partial(_ffn_kernel, tm=tm, halo=halo, shift=shift),
        grid=(n, ni, nj),
        in_specs=[pl.BlockSpec((1, tm, d), lambda b, i, j: (b, i, 0)),
                  pl.BlockSpec((1, d), lambda b, i, j: (0, 0)),
                  pl.BlockSpec((d, tf), lambda b, i, j: (0, j)),
                  pl.BlockSpec((d, tf), lambda b, i, j: (0, j + nj)),
                  pl.BlockSpec((FFN_CONV, tf), lambda b, i, j: (0, j)),
                  pl.BlockSpec((FFN_CONV, tf), lambda b, i, j: (0, j + nj)),
                  pl.BlockSpec((tf, d), lambda b, i, j: (j, 0)),
                  pl.BlockSpec((1, d), lambda b, i, j: (0, 0)),
                  pl.BlockSpec((1, halo, tf), lambda b, i, j: (b, 0, j)),
                  pl.BlockSpec((1, halo, tf), lambda b, i, j: (b, 0, j + nj))],
        out_specs=[pl.BlockSpec((1, tm, d), lambda b, i, j: (b, i, 0)),
                   pl.BlockSpec((1, 1, halo, tf), lambda b, i, j: (b, i, 0, j)),
                   pl.BlockSpec((1, 1, halo, tf), lambda b, i, j: (b, i, 0, j))],
        out_shape=[jax.ShapeDtypeStruct((n, t, d), F32),
                   jax.ShapeDtypeStruct((n, ni, halo, dff), F32),
                   jax.ShapeDtypeStruct((n, ni, halo, dff), F32)],
        scratch_shapes=scratch,
        compiler_params=_cparams("parallel", "arbitrary", "arbitrary"), name="conv_ffn",
    )(x, g_pre.reshape(1, d), w_up, w_up, conv_w, conv_w, w_down, g_post.reshape(1, d), init, init)
    return out, jnp.concatenate([tail_g[:, ni - 1], tail_v[:, ni - 1]], axis=-1)


def _ple_kernel(r_ref, p_ref, wp_ref, wg_ref, o_ref):
    r = r_ref[...]
    o_ref[...] = r + _dot(p_ref[...], wp_ref[...]) * _sigmoid(_dot(r, wg_ref[...]))


def _ple(r, p, w_proj, w_gate, tm):
    m, d = r.shape
    pd = p.shape[1]
    return pl.pallas_call(
        _ple_kernel, grid=(m // tm,),
        in_specs=[pl.BlockSpec((tm, d), lambda i: (i, 0)),
                  pl.BlockSpec((tm, pd), lambda i: (i, 0)),
                  pl.BlockSpec((pd, d), lambda i: (0, 0)),
                  pl.BlockSpec((d, d), lambda i: (0, 0))],
        out_specs=pl.BlockSpec((tm, d), lambda i: (i, 0)),
        out_shape=jax.ShapeDtypeStruct((m, d), F32),
        compiler_params=_cparams("parallel"), name="ple",
    )(r, p, w_proj, w_gate)


def _pad_cols(w, width):
    return jnp.pad(w, ((0, 0), (0, width - w.shape[1])))


def _prep_weights(w_in_even, w_out_even, w_in_odd, w_out_odd, gla_gate_w2, ffn_w_up, ffn_w_down,
                  ple_w_proj, ple_w_gate, dn_a_log, dn_dt_bias):
    a_qkv = 3 * A_HEADS * HEAD_DIM
    b_qkv = 3 * B_HEADS * B_DK
    ab = a_qkv + b_qkv
    nb = 2 * B_HEADS
    w0 = jnp.concatenate([w_in_even[0][:, :ab], w_in_even[0][:, ab + nb:], w_in_even[0][:, ab:ab + nb]], axis=1)
    w0 = _pad_cols(w0, -(-w0.shape[1] // PROJ_TN) * PROJ_TN).astype(BF16)
    d = w_in_odd.shape[1]
    qkv = 2 * (d // 2) + d
    w1 = jnp.concatenate([w_in_odd[0][:, :qkv], w_in_odd[0][:, qkv + C_GATE_RANK:],
                          w_in_odd[0][:, qkv:qkv + C_GATE_RANK]], axis=1)
    w1 = _pad_cols(w1, -(-w1.shape[1] // PROJ_TN) * PROJ_TN).astype(BF16)
    w2 = jnp.pad(gla_gate_w2[0], ((0, LANES - C_GATE_RANK), (0, 0))).astype(BF16)
    adt_row = jnp.pad(jnp.stack([dn_a_log[0], dn_dt_bias[0]]), ((0, 0), (B_HEADS, LANES - 2 * B_HEADS)))
    adt_col = jnp.stack([dn_a_log[0], dn_dt_bias[0]], axis=1)
    return dict(w0=w0, w1=w1, w2=w2, adt_row=adt_row, adt_col=adt_col,
                w_out_even=w_out_even[0].astype(BF16), w_out_odd=w_out_odd[0].astype(BF16),
                w_up=ffn_w_up.astype(BF16), w_down=ffn_w_down.astype(BF16),
                w_proj=ple_w_proj.astype(BF16), w_gate=ple_w_gate.astype(BF16))


def _layer_tail(r, p, layer, wt, prm, ffn_init, n, t, tm, ffn_tm, shift):
    d = r.shape[1]
    r, tail = _ffn(r.reshape(n, t, d), prm["norm_ffn_pre"][layer], wt["w_up"][layer], prm["ffn_conv_w"][layer],
                   wt["w_down"][layer], prm["norm_ffn_post"][layer], ffn_init, ffn_tm, 512, shift)
    r = _ple(r.reshape(n * t, d), p, wt["w_proj"][layer], wt["w_gate"][layer], tm)
    return r, tail


def _prompt_group(x, p, wt, prm):
    n, t, d = x.shape
    m = n * t
    tm = 256
    dff2 = prm["ffn_conv_w"].shape[-1]
    r = x.reshape(m, d)
    proj = _rms_matmul(r, prm["norm_mix_pre"][0], wt["w0"], 512, PROJ_TN)
    pv = proj.reshape(n, t, -1)
    o_a = _attn_prompt(pv, n, t).reshape(m, -1)
    width = 3 * B_HEADS * B_DK
    o_b, dn_state = _dn_prompt(pv, prm["dn_conv_w"][0], wt["adt_row"], jnp.zeros((n, 8, width), F32),
                               jnp.zeros((n, B_HEADS, B_DK, B_DK), F32))
    r = _mix_even(o_a, o_b.reshape(m, -1), proj, prm["dn_norm_w"][0], wt["w_out_even"], r,
                  prm["norm_mix_post"][0], tm)
    keep = min(A_WMAX, t)
    hd = A_HEADS * HEAD_DIM
    win_k = pv[:, t - keep:, hd:2 * hd].reshape(n, keep, A_HEADS, HEAD_DIM)
    win_v = pv[:, t - keep:, 2 * hd:3 * hd].reshape(n, keep, A_HEADS, HEAD_DIM)
    dn_conv = pv[:, t - (B_CONV - 1):, width:2 * width]
    ffn_zero = jnp.zeros((n, 8, dff2), F32)
    r, tail0 = _layer_tail(r, p[0].reshape(m, -1), 0, wt, prm, ffn_zero, n, t, tm, 512, 1)
    proj1 = _rms_matmul(r, prm["norm_mix_pre"][1], wt["w1"], 512, PROJ_TN)
    o_c, gla_state = _gla(proj1.reshape(n, t, -1), wt["w2"], prm["gla_gate_b"][0].reshape(1, -1),
                          jnp.zeros((n, C_HEADS, d // (2 * C_HEADS), d // C_HEADS), F32))
    r = _mix_odd(o_c.reshape(m, -1), proj1, prm["gla_norm_w"][0], wt["w_out_odd"], r, prm["norm_mix_post"][1], tm)
    r, tail1 = _layer_tail(r, p[1].reshape(m, -1), 1, wt, prm, ffn_zero, n, t, tm, 512, 1)
    ffn_conv = jnp.stack([tail0[:, 8 - (FFN_CONV - 1):], tail1[:, 8 - (FFN_CONV - 1):]])
    return (r.reshape(n, t, d), win_k[None], win_v[None], dn_conv[None], dn_state[None], gla_state[None], ffn_conv)


def _sample_group(x, p, cache_k, cache_v, dn_conv_state, dn_state, gla_state, ffn_state, wt, prm):
    n, t, d = x.shape
    m = n * t
    tm = 256
    r = x.reshape(m, d)

    def time_major(a):
        return a.reshape(n, t, -1).transpose(1, 0, 2).reshape(1, m, -1)

    def seq_major(a):
        return a.reshape(t, n, -1).transpose(1, 0, 2).reshape(m, -1)

    def ffn_layer(r, layer):
        init = ffn_state[layer].transpose(1, 0, 2).reshape(1, (FFN_CONV - 1) * n, -1)
        rt, tail = _ffn(time_major(r), prm["norm_ffn_pre"][layer], wt["w_up"][layer], prm["ffn_conv_w"][layer],
                        wt["w_down"][layer], prm["norm_ffn_post"][layer], init, m, 512, n)
        r = _ple(seq_major(rt), p[layer].reshape(m, -1), wt["w_proj"][layer], wt["w_gate"][layer], tm)
        return r, tail.reshape(FFN_CONV - 1, n, -1).transpose(1, 0, 2)

    proj = _rms_matmul(r, prm["norm_mix_pre"][0], wt["w0"], m, PROJ_TN)
    pv = proj.reshape(n, t, -1)
    width = 3 * B_HEADS * B_DK
    heads = pv.reshape(n, t, -1, HEAD_DIM)
    o_a = _attn_sample(heads, cache_k[0], cache_v[0])
    ba_col = (2 * width + B_HEADS * B_DK) // LANES
    bat = heads[:, :, ba_col].transpose(0, 2, 1)
    o_b, dn_new = _dn_sample(heads, bat, prm["dn_conv_w"][0].reshape(B_CONV, -1, B_DK), wt["adt_col"],
                             dn_conv_state[0].reshape(n, B_CONV - 1, -1, B_DK),
                             dn_state[0].reshape(n, B_HEADS * B_DK, B_DK))
    r = _mix_even(o_a.reshape(m, -1), o_b.reshape(m, -1), proj, prm["dn_norm_w"][0], wt["w_out_even"], r,
                  prm["norm_mix_post"][0], tm)
    win_k = heads[:, :, A_HEADS:2 * A_HEADS]
    win_v = heads[:, :, 2 * A_HEADS:3 * A_HEADS]
    dn_conv = pv[:, t - (B_CONV - 1):, width:2 * width]
    r, tail0 = ffn_layer(r, 0)
    proj1 = _rms_matmul(r, prm["norm_mix_pre"][1], wt["w1"], m, PROJ_TN)
    o_c, gla_new = _gla(proj1.reshape(n, t, -1), wt["w2"], prm["gla_gate_b"][0].reshape(1, -1), gla_state[0])
    r = _mix_odd(o_c.reshape(m, -1), proj1, prm["gla_norm_w"][0], wt["w_out_odd"], r, prm["norm_mix_post"][1], tm)
    r, tail1 = ffn_layer(r, 1)
    return (r.reshape(n, t, d), win_k[None], win_v[None], dn_conv[None],
            dn_new.reshape(dn_state.shape), gla_new[None], jnp.stack([tail0, tail1]))


def kernel(x_prompt, x_sample, cache_win_k, cache_win_v, state_dn_conv, state_dn, state_gla, state_ffn_conv, p_prompt, p_sample, norm_mix_pre, norm_mix_post, norm_ffn_pre, norm_ffn_post, w_in_even, w_out_even, dn_conv_w, dn_a_log, dn_dt_bias, dn_norm_w, w_in_odd, gla_gate_w2, gla_gate_b, gla_norm_w, w_out_odd, ffn_w_up, ffn_conv_w, ffn_w_down, ple_w_proj, ple_w_gate):
    wt = _prep_weights(w_in_even, w_out_even, w_in_odd, w_out_odd, gla_gate_w2, ffn_w_up, ffn_w_down,
                       ple_w_proj, ple_w_gate, dn_a_log, dn_dt_bias)
    prm = dict(norm_mix_pre=norm_mix_pre, norm_mix_post=norm_mix_post, norm_ffn_pre=norm_ffn_pre,
               norm_ffn_post=norm_ffn_post, dn_conv_w=dn_conv_w, dn_norm_w=dn_norm_w, gla_gate_b=gla_gate_b,
               gla_norm_w=gla_norm_w, ffn_conv_w=ffn_conv_w)
    yp, kp, vp, dcp, dsp, gsp, fcp = _prompt_group(x_prompt, p_prompt, wt, prm)
    ys, ks, vs, dcs, dss, gss, fcs = _sample_group(x_sample, p_sample, cache_win_k, cache_win_v, state_dn_conv,
                                                   state_dn, state_gla, state_ffn_conv, wt, prm)
    return (yp, ys, kp, vp, dcp, dsp, gsp, fcp, ks, vs, dcs, dss, gss, fcs)
```

```python
import functools
import math

import jax
import jax.numpy as jnp
from jax import lax
from jax.experimental import pallas as pl
from jax.experimental.pallas import tpu as pltpu

F32 = jnp.float32
BF16 = jnp.bfloat16
EPS = 1e-6
NEG = -1e30
HIGHEST = lax.Precision.HIGHEST

LANES = 128
V7X_VMEM_LIMIT_BYTES = 56 * 2**20

HEAD_DIM = 128
A_HEADS = 8
A_KEYS = 128
A_DILATIONS = (1, 4, 16)
A_WMAX = 2048
B_HEADS = 8
B_DK = 128
B_CONV = 4
C_HEADS = 4
C_GATE_RANK = 16
C_GATE_TAU = 16.0
FFN_CONV = 3
DN_CHUNK = 64
GLA_CHUNK = 64
GLA_SUB = 16
PROJ_TN = 1280
PROMPT_PROJ_TM = 1024
PROMPT_FFN_TM = 1024
PROMPT_FFN_TF = 512
FFN_SUB = 512
SAMPLE_FFN_TF = 512


def _cparams(*sem):
    return pltpu.CompilerParams(dimension_semantics=sem, vmem_limit_bytes=V7X_VMEM_LIMIT_BYTES)


def _dot(a, b):
    return jnp.dot(a.astype(BF16), b.astype(BF16), preferred_element_type=F32)


def _dot_nt(a, b):
    return lax.dot_general(a.astype(BF16), b.astype(BF16), (((1,), (1,)), ((), ())),
                           preferred_element_type=F32)


def _dot_tn(a, b):
    return lax.dot_general(a.astype(BF16), b.astype(BF16), (((0,), (0,)), ((), ())),
                           preferred_element_type=F32)


def _bmm(a, b):
    return lax.dot_general(a.astype(BF16), b.astype(BF16), (((2,), (1,)), ((0,), (0,))),
                           preferred_element_type=F32)


def _bmm_nt(a, b):
    return lax.dot_general(a.astype(BF16), b.astype(BF16), (((2,), (2,)), ((0,), (0,))),
                           preferred_element_type=F32)


def _bmm_tn(a, b):
    return lax.dot_general(a.astype(BF16), b.astype(BF16), (((1,), (1,)), ((0,), (0,))),
                           preferred_element_type=F32)


def _dot_f32(a, b):
    return jnp.dot(a, b, precision=HIGHEST, preferred_element_type=F32)


def _dot_tn_f32(a, b):
    return lax.dot_general(a, b, (((0,), (0,)), ((), ())), precision=HIGHEST,
                           preferred_element_type=F32)


def _rms(x, g):
    return x * lax.rsqrt(jnp.mean(x * x, axis=-1, keepdims=True) + EPS) * g


def _sigmoid(x):
    return 1.0 / (1.0 + jnp.exp(-x))


def _softplus(x):
    return jnp.maximum(x, 0.0) + jnp.log(1.0 + jnp.exp(-jnp.abs(x)))


def _gelu_tanh(x):
    return 0.5 * x * (1.0 + jnp.tanh(math.sqrt(2.0 / math.pi) * (x + 0.044715 * (x * x * x))))


def _rms_matmul_kernel(x_ref, g_ref, w_ref, o_ref, h_ref):
    @pl.when(pl.program_id(1) == 0)
    def _():
        h_ref[...] = _rms(x_ref[...], g_ref[...]).astype(BF16)

    o_ref[...] = jnp.dot(h_ref[...], w_ref[...], preferred_element_type=F32)


def _rms_matmul(x, g, w, tm, tn):
    m, k = x.shape
    n = w.shape[1]
    return pl.pallas_call(
        _rms_matmul_kernel,
        grid=(m // tm, n // tn),
        in_specs=[pl.BlockSpec((tm, k), lambda i, j: (i, 0)),
                  pl.BlockSpec((1, k), lambda i, j: (0, 0)),
                  pl.BlockSpec((k, tn), lambda i, j: (0, j))],
        out_specs=pl.BlockSpec((tm, tn), lambda i, j: (i, j)),
        out_shape=jax.ShapeDtypeStruct((m, n), F32),
        scratch_shapes=[pltpu.VMEM((tm, k), BF16)],
        compiler_params=_cparams("parallel", "arbitrary"),
        name="rms_matmul",
    )(x, g.reshape(1, k), w)


ATTN_SB = A_WMAX


def _attn_prompt_kernel(q_ref, kc_ref, kp_ref, vc_ref, vp_ref, o_ref, o_s, l_s, *, scale):
    has_prev = pl.program_id(2) > 0
    row = lax.broadcasted_iota(jnp.int32, (A_KEYS, 2 * A_KEYS), 0)
    col = lax.broadcasted_iota(jnp.int32, (A_KEYS, 2 * A_KEYS), 1)
    cur_ok = (col >= A_KEYS) & (col - A_KEYS <= row)
    prev_ok = (col < A_KEYS) & (col >= row)

    def tile(ref, start, d):
        if d == 1:
            return ref[0, pl.ds(start, A_KEYS), :]
        return ref[0, pl.ds(start, A_KEYS, stride=d), :]

    for g, d in enumerate(A_DILATIONS):
        span = A_KEYS * d
        for r in range(d):
            for j in range(ATTN_SB // span):
                start = r + span * j
                q = tile(q_ref, start, d) * scale
                if j > 0:
                    kp, vp = tile(kc_ref, start - span, d), tile(vc_ref, start - span, d)
                    ok = cur_ok | prev_ok
                else:
                    kp, vp = tile(kp_ref, ATTN_SB - span + r, d), tile(vp_ref, ATTN_SB - span + r, d)
                    ok = cur_ok | (prev_ok & has_prev)
                s = _dot_nt(q, jnp.concatenate([kp, tile(kc_ref, start, d)], axis=0))
                s = jnp.where(ok, s, NEG)
                m = jnp.max(s, axis=-1, keepdims=True)
                p = jnp.exp(s - m)
                l = jnp.sum(p, axis=-1, keepdims=True)
                o = _dot(p, jnp.concatenate([vp, tile(vc_ref, start, d)], axis=0)) / l
                lse = jnp.broadcast_to(m + jnp.log(l), (A_KEYS, HEAD_DIM))
                if d == 1:
                    o_s[g, pl.ds(start, A_KEYS), :] = o
                    l_s[g, pl.ds(start, A_KEYS), :] = lse
                else:
                    o_s[g, pl.ds(start, A_KEYS, stride=d), :] = o
                    l_s[g, pl.ds(start, A_KEYS, stride=d), :] = lse
    la, lb, lc = l_s[0], l_s[1], l_s[2]
    m = jnp.maximum(jnp.maximum(la, lb), lc)
    wa, wb, wc = jnp.exp(la - m), jnp.exp(lb - m), jnp.exp(lc - m)
    o_ref[0] = (wa * o_s[0] + wb * o_s[1] + wc * o_s[2]) / (wa + wb + wc)


def _attn_prompt(proj, n, t):
    assert t % ATTN_SB == 0 and len(A_DILATIONS) == 3
    cur = lambda off: (lambda b, h, i: (b, i, off + h))
    prev = lambda off: (lambda b, h, i: (b, jnp.maximum(i - 1, 0), off + h))
    blk = (1, ATTN_SB, HEAD_DIM)
    return pl.pallas_call(
        functools.partial(_attn_prompt_kernel, scale=HEAD_DIM ** -0.5),
        grid=(n, A_HEADS, t // ATTN_SB),
        in_specs=[pl.BlockSpec(blk, cur(0)),
                  pl.BlockSpec(blk, cur(A_HEADS)), pl.BlockSpec(blk, prev(A_HEADS)),
                  pl.BlockSpec(blk, cur(2 * A_HEADS)), pl.BlockSpec(blk, prev(2 * A_HEADS))],
        out_specs=pl.BlockSpec(blk, lambda b, h, i: (b, i, h)),
        out_shape=jax.ShapeDtypeStruct((n, t, A_HEADS * HEAD_DIM), F32),
        scratch_shapes=[pltpu.VMEM((3, ATTN_SB, HEAD_DIM), F32), pltpu.VMEM((3, ATTN_SB, HEAD_DIM), F32)],
        compiler_params=_cparams("parallel", "parallel", "arbitrary"), name="attn_prompt",
    )(proj, proj, proj, proj, proj)


def _attn_sample_kernel(x_ref, k1_ref, k2_ref, v1_ref, v2_ref, o_ref, *, scale):
    tn = 4
    nq = tn * A_HEADS
    nk = A_KEYS * A_HEADS
    x = x_ref[0]
    q = (x[:, 0:A_HEADS] * scale).reshape(nq, HEAD_DIM)
    knew = x[:, A_HEADS:2 * A_HEADS].reshape(nq, HEAD_DIM)
    vnew = x[:, 2 * A_HEADS:3 * A_HEADS].reshape(nq, HEAD_DIM)

    def grid_masks(cols):
        qrow = lax.broadcasted_iota(jnp.int32, (nq, cols), 0)
        kcol = lax.broadcasted_iota(jnp.int32, (nq, cols), 1)
        return (qrow % A_HEADS) == (kcol % A_HEADS), qrow // A_HEADS, kcol // A_HEADS

    same_head, qt, key = grid_masks(nk)
    flat = lambda a: a.reshape(nk, HEAD_DIM)
    dense = slice(A_KEYS - A_KEYS // 4, A_KEYS)
    segs = [(flat(k1_ref[0, dense]), flat(v1_ref[0, dense]), jnp.where(same_head & (key >= qt), 1.0, 0.0))]
    for kr, vr in ((k1_ref, v1_ref), (k2_ref, v2_ref)):
        for t in range(tn):
            segs.append((flat(kr[0, :, t]), flat(vr[0, :, t]), jnp.where(same_head & (qt == t), 1.0, 0.0)))
    same_head, qt, kt = grid_masks(nq)
    segs.append((knew, vnew, jnp.where(same_head & (kt == qt), float(len(A_DILATIONS)),
                                       jnp.where(same_head & (kt < qt), 1.0, 0.0))))

    scores = [_dot_nt(q, k) for k, _, _ in segs]
    m = None
    for s, (_, _, mult) in zip(scores, segs):
        ms = jnp.max(jnp.where(mult > 0.0, s, NEG), axis=-1, keepdims=True)
        m = ms if m is None else jnp.maximum(m, ms)
    probs = [jnp.where(mult > 0.0, jnp.exp(s - m), 0.0) * mult for s, (_, _, mult) in zip(scores, segs)]
    l = sum(jnp.sum(p, axis=-1, keepdims=True) for p in probs)
    acc = sum(_dot(p, v) for p, (_, v, _) in zip(probs, segs))
    o_ref[0] = (acc / l).reshape(tn, A_HEADS, HEAD_DIM)


def _attn_sample(x4, cache_k, cache_v):
    n, tn = x4.shape[0], x4.shape[1]
    w = cache_k.shape[1]
    assert w == A_WMAX and tn == 4
    views = []
    for cache in (cache_k, cache_v):
        views += [cache.reshape(n, w // 4, 4, A_HEADS, HEAD_DIM), cache.reshape(n, w // 16, 16, A_HEADS, HEAD_DIM)]
    k1, k2, v1, v2 = views
    s1 = pl.BlockSpec((1, A_KEYS, 4, A_HEADS, HEAD_DIM), lambda b: (b, w // 4 // A_KEYS - 1, 0, 0, 0))
    s2 = pl.BlockSpec((1, A_KEYS, 4, A_HEADS, HEAD_DIM), lambda b: (b, 0, 0, 0, 0))
    return pl.pallas_call(
        functools.partial(_attn_sample_kernel, scale=HEAD_DIM ** -0.5),
        grid=(n,),
        in_specs=[pl.BlockSpec((1, tn, 3 * A_HEADS, HEAD_DIM), lambda b: (b, 0, 0, 0)), s1, s2, s1, s2],
        out_specs=pl.BlockSpec((1, tn, A_HEADS, HEAD_DIM), lambda b: (b, 0, 0, 0)),
        out_shape=jax.ShapeDtypeStruct((n, tn, A_HEADS, HEAD_DIM), F32),
        compiler_params=_cparams("parallel"), name="attn_sample",
    )(x4, k1, k2, v1, v2)


def _unit_lower_inverse_minus_identity(low):
    c = low.shape[-1]
    base = min(16, c)
    row = lax.broadcasted_iota(jnp.int32, (1, c, c), 1)
    col = lax.broadcasted_iota(jnp.int32, (1, c, c), 2)
    nil = jnp.where((row // base) == (col // base), -low, 0.0)
    q = nil
    pw = nil
    for _ in range(int(math.log2(base)) - 1):
        pw = _bmm(pw, pw)
        q = q + pw + _bmm(q, pw)
    b = base
    while b < c:
        sib = ((row // (2 * b)) == (col // (2 * b))) & ((row // b) != (col // b))
        off = jnp.where(sib, low, 0.0)
        t = off + _bmm(q, off)
        q = q - t - _bmm(t, q)
        b *= 2
    return q


def _dn_prompt_kernel(x_ref, ba_ref, cw_ref, adt_ref, cinit_ref, sinit_ref, o_ref, sout_ref,
                      xbuf, s_ref, *, c):
    ci = pl.program_id(1)
    halo = 8
    hk = B_HEADS * B_DK

    @pl.when(ci == 0)
    def _():
        xbuf[0:halo, :] = cinit_ref[0]
        s_ref[...] = sinit_ref[0]

    xbuf[halo:halo + c, :] = x_ref[0]
    cw = cw_ref[...]
    y = cw[0:1, :] * xbuf[pl.ds(halo - B_CONV + 1, c), :]
    for i in range(1, B_CONV):
        y = y + cw[i:i + 1, :] * xbuf[pl.ds(halo - B_CONV + 1 + i, c), :]
    xbuf[0:halo, :] = xbuf[c:c + halo, :]
    cq = y * _sigmoid(y)

    ba = ba_ref[0]
    beta_all = _sigmoid(ba)
    g_all = -jnp.exp(adt_ref[0:1, :]) * _softplus(ba + adt_ref[1:2, :])
    row = lax.broadcasted_iota(jnp.int32, (c, c), 0)
    col = lax.broadcasted_iota(jnp.int32, (c, c), 1)
    gcum_col = _dot_f32((row >= col).astype(F32), g_all)
    gcum_row = _dot_tn_f32(g_all, (row <= col).astype(F32))

    def heads(off):
        return jnp.stack([cq[:, off + h * B_DK:off + (h + 1) * B_DK] for h in range(B_HEADS)])

    q, k, v = heads(0), heads(hk), heads(2 * hk)
    q = q * lax.rsqrt(jnp.sum(q * q, axis=-1, keepdims=True) + EPS) * (B_DK ** -0.5)
    k = k * lax.rsqrt(jnp.sum(k * k, axis=-1, keepdims=True) + EPS)
    beta = jnp.stack([beta_all[:, h:h + 1] for h in range(B_HEADS)])
    gc = jnp.stack([gcum_col[:, B_HEADS + h:B_HEADS + h + 1] for h in range(B_HEADS)])
    gr = jnp.stack([gcum_row[B_HEADS + h:B_HEADS + h + 1, :] for h in range(B_HEADS)])
    decay = jnp.exp(jnp.where((row >= col)[None], gc - gr, NEG))
    kb = k * beta
    low = jnp.where((row > col)[None], _bmm_nt(kb, k) * decay, 0.0)
    qinv = _unit_lower_inverse_minus_identity(low)
    eg = jnp.exp(gc)
    rhs = jnp.concatenate([v * beta, kb * eg], axis=2)
    sol = rhs + _bmm(qinv, rhs)
    u = sol[:, :, 0:B_DK]
    w = sol[:, :, B_DK:2 * B_DK]
    attn = _bmm_nt(q, k) * decay
    glast = gc[:, c - 1:c, :]
    kd = k * jnp.exp(glast - gc)
    s = s_ref[...]
    v_new = u - _bmm(w, s)
    o = _bmm(q * eg, s) + _bmm(attn, v_new)
    s_ref[...] = s * jnp.exp(glast) + _bmm_tn(kd, v_new)
    o_ref[0] = jnp.concatenate([o[h] for h in range(B_HEADS)], axis=1)

    @pl.when(ci == pl.num_programs(1) - 1)
    def _():
        sout_ref[0] = s_ref[...]


def _dn_prompt(proj, conv_w, adt, conv_init, s_init):
    n, t, _ = proj.shape
    c = math.gcd(t, DN_CHUNK)
    width = 3 * B_HEADS * B_DK
    return pl.pallas_call(
        functools.partial(_dn_prompt_kernel, c=c),
        grid=(n, t // c),
        in_specs=[pl.BlockSpec((1, c, width), lambda b, i: (b, i, 1)),
                  pl.BlockSpec((1, c, LANES), lambda b, i: (b, i, (2 * width + B_HEADS * B_DK) // LANES)),
                  pl.BlockSpec((B_CONV, width), lambda b, i: (0, 0)),
                  pl.BlockSpec((2, LANES), lambda b, i: (0, 0)),
                  pl.BlockSpec((1, 8, width), lambda b, i: (b, 0, 0)),
                  pl.BlockSpec((1, B_HEADS, B_DK, B_DK), lambda b, i: (b, 0, 0, 0))],
        out_specs=[pl.BlockSpec((1, c, B_HEADS * B_DK), lambda b, i: (b, i, 0)),
                   pl.BlockSpec((1, B_HEADS, B_DK, B_DK), lambda b, i: (b, 0, 0, 0))],
        out_shape=[jax.ShapeDtypeStruct((n, t, B_HEADS * B_DK), F32),
                   jax.ShapeDtypeStruct((n, B_HEADS, B_DK, B_DK), F32)],
        scratch_shapes=[pltpu.VMEM((8 + c, width), F32), pltpu.VMEM((B_HEADS, B_DK, B_DK), F32)],
        compiler_params=_cparams("parallel", "arbitrary"), name="deltanet_prompt",
    )(proj, proj, conv_w, adt, conv_init, s_init)


DN_SAMPLE_SEQS = 4


def _dn_sample_kernel(x_ref, bat_ref, cw_ref, adt_ref, cs_ref, sinit_ref, o_ref, s_ref):
    tn = 4
    hk = B_HEADS * B_DK
    seqs = range(DN_SAMPLE_SEQS)
    xp = jnp.concatenate([cs_ref[...], x_ref[...]], axis=1)
    cw = cw_ref[...]
    y = cw[0] * xp[:, 0:tn]
    for i in range(1, B_CONV):
        y = y + cw[i] * xp[:, i:i + tn]
    cq = y * _sigmoid(y)
    q = cq[:, :, 0:B_HEADS]
    k = cq[:, :, B_HEADS:2 * B_HEADS]
    v = cq[:, :, 2 * B_HEADS:3 * B_HEADS]
    q = q * lax.rsqrt(jnp.sum(q * q, axis=-1, keepdims=True) + EPS) * (B_DK ** -0.5)
    k = k * lax.rsqrt(jnp.sum(k * k, axis=-1, keepdims=True) + EPS)
    bat = bat_ref[...]
    beta = _sigmoid(bat[:, 0:B_HEADS, :])
    a = jnp.exp(-jnp.exp(adt_ref[:, 0:1]) * _softplus(bat[:, B_HEADS:2 * B_HEADS, :] + adt_ref[:, 1:2]))
    lane_head = lax.broadcasted_iota(jnp.int32, (B_HEADS, hk), 1) // B_DK
    head_mask = lane_head == lax.broadcasted_iota(jnp.int32, (B_HEADS, hk), 0)

    def block_diag(x):
        return jnp.where(head_mask, jnp.concatenate([x] * B_HEADS, axis=1), 0.0)

    s_ref[...] = sinit_ref[...]
    for t in range(tn):
        kbd = [block_diag(k[b, t]) for b in seqs]
        ks = [_dot(kbd[b], s_ref[b]) for b in seqs]
        w = [beta[b, :, t:t + 1] * (v[b, t] - a[b, :, t:t + 1] * ks[b]) for b in seqs]
        upd = [_dot_tn(kbd[b], w[b]) for b in seqs]
        for b in seqs:
            for h in range(B_HEADS):
                rows = slice(h * B_DK, (h + 1) * B_DK)
                s_ref[b, rows, :] = s_ref[b, rows, :] * a[b, h:h + 1, t:t + 1] + upd[b][rows, :]
        for b in seqs:
            o_ref[b, t] = _dot(block_diag(q[b, t]), s_ref[b])


def _dn_sample(x4, bat, conv_w, adt_col, conv_state, s_init):
    n = x4.shape[0]
    nb = DN_SAMPLE_SEQS
    rows = 3 * B_HEADS
    hk = B_HEADS * B_DK
    return pl.pallas_call(
        _dn_sample_kernel,
        grid=(n // nb,),
        in_specs=[pl.BlockSpec((nb, 4, rows, B_DK), lambda b: (b, 0, 1, 0)),
                  pl.BlockSpec((nb, LANES, 4), lambda b: (b, 0, 0)),
                  pl.BlockSpec((B_CONV, rows, B_DK), lambda b: (0, 0, 0)),
                  pl.BlockSpec((B_HEADS, 2), lambda b: (0, 0)),
                  pl.BlockSpec((nb, B_CONV - 1, rows, B_DK), lambda b: (b, 0, 0, 0)),
                  pl.BlockSpec((nb, hk, B_DK), lambda b: (b, 0, 0))],
        out_specs=[pl.BlockSpec((nb, 4, B_HEADS, B_DK), lambda b: (b, 0, 0, 0)),
                   pl.BlockSpec((nb, hk, B_DK), lambda b: (b, 0, 0))],
        out_shape=[jax.ShapeDtypeStruct((n, 4, B_HEADS, B_DK), F32),
                   jax.ShapeDtypeStruct((n, hk, B_DK), F32)],
        compiler_params=_cparams("parallel"), name="deltanet_sample",
    )(x4, bat, conv_w, adt_col, conv_state, s_init)


def _gla_kernel(q_ref, k_ref, v_ref, glr_ref, w2_ref, gb_ref, sinit_ref, o_ref, sout_ref,
                s_ref, *, c_real, c, dk, dv):
    ci = pl.program_id(1)
    sub = min(GLA_SUB, c)

    @pl.when(ci == 0)
    def _():
        s_ref[...] = sinit_ref[0]

    def load(ref):
        x = ref[0]
        if c_real < c:
            x = jnp.concatenate([x, jnp.zeros((c - c_real, x.shape[1]), F32)], axis=0)
        return x

    qa, ka, va, glr = load(q_ref), load(k_ref), load(v_ref), load(glr_ref)
    row = lax.broadcasted_iota(jnp.int32, (c, c), 0)
    col = lax.broadcasted_iota(jnp.int32, (c, c), 1)
    tri = (row >= col).astype(F32)
    row_valid = lax.broadcasted_iota(jnp.int32, (c, 1), 0) < c_real
    sub_row = lax.broadcasted_iota(jnp.int32, (sub, 1), 0)
    ones = jnp.ones((c, LANES), F32)
    pre = _dot(glr, w2_ref[...]) + gb_ref[...]
    la_all = jnp.where(row_valid, (jnp.minimum(pre, 0.0) - jnp.log(1.0 + jnp.exp(-jnp.abs(pre)))) / C_GATE_TAU, 0.0)
    gc_all = _dot_f32(tri, la_all)
    outs = []
    for h in range(C_HEADS):
        la = la_all[:, h * dk:(h + 1) * dk]
        gc = gc_all[:, h * dk:(h + 1) * dk]
        q = qa[:, h * dk:(h + 1) * dk] * (dk ** -0.5)
        k = ka[:, h * dk:(h + 1) * dk]
        v = va[:, h * dv:(h + 1) * dv]
        blocks = []
        for b in range(c // sub):
            lo = b * sub
            qb = q[lo:lo + sub, :]
            gblk = gc[lo:lo + sub, :]
            acc = jnp.zeros((sub, dv), F32)
            for jj in range(sub):
                j = lo + jj
                e = jnp.exp(jnp.minimum(gblk - gc[j:j + 1, :], 0.0))
                a = jnp.sum(qb * k[j:j + 1, :] * e, axis=-1, keepdims=True)
                acc = acc + jnp.where(sub_row >= jj, a, 0.0) * v[j:j + 1, :]
            blocks.append(acc)
        o = jnp.concatenate(blocks, axis=0)
        if c > sub:
            blocks = [jnp.zeros((sub, dv), F32)]
            for b in range(1, c // sub):
                lo = b * sub
                ref_pt = gc[lo:lo + 1, :]
                qsc = q[lo:lo + sub, :] * jnp.exp(gc[lo:lo + sub, :] - ref_pt)
                ksc = k[0:lo, :] * jnp.exp(ref_pt - gc[0:lo, :])
                blocks.append(_dot(_dot_nt(qsc, ksc), v[0:lo, :]))
            o = o + jnp.concatenate(blocks, axis=0)
        s = s_ref[h]
        o = o + _dot(q * jnp.exp(gc), s)
        glast = gc[c - 1:c, :]
        kd = k * jnp.exp(glast - gc)
        gl_col = jnp.exp(_dot_tn_f32(la, ones))[:, 0:1]
        s_ref[h] = s * gl_col + _dot_tn(kd, v)
        outs.append(o[0:c_real, :])
    o_ref[0] = jnp.concatenate(outs, axis=1)

    @pl.when(ci == pl.num_programs(1) - 1)
    def _():
        sout_ref[0] = s_ref[...]


def _gla(proj, w2, gb, s_init):
    n, t, _ = proj.shape
    dk, dv = s_init.shape[2], s_init.shape[3]
    c_real = math.gcd(t, GLA_CHUNK)
    c = max(c_real, 8)
    hk, hv = C_HEADS * dk, C_HEADS * dv
    return pl.pallas_call(
        functools.partial(_gla_kernel, c_real=c_real, c=c, dk=dk, dv=dv),
        grid=(n, t // c_real),
        in_specs=[pl.BlockSpec((1, c_real, hk), lambda b, i: (b, i, 0)),
                  pl.BlockSpec((1, c_real, hk), lambda b, i: (b, i, 1)),
                  pl.BlockSpec((1, c_real, hv), lambda b, i: (b, i, 2 * hk // hv)),
                  pl.BlockSpec((1, c_real, LANES), lambda b, i: (b, i, (2 * hk + 2 * hv) // LANES)),
                  pl.BlockSpec((LANES, hk), lambda b, i: (0, 0)),
                  pl.BlockSpec((1, hk), lambda b, i: (0, 0)),
                  pl.BlockSpec((1, C_HEADS, dk, dv), lambda b, i: (b, 0, 0, 0))],
        out_specs=[pl.BlockSpec((1, c_real, hv), lambda b, i: (b, i, 0)),
                   pl.BlockSpec((1, C_HEADS, dk, dv), lambda b, i: (b, 0, 0, 0))],
        out_shape=[jax.ShapeDtypeStruct((n, t, hv), F32),
                   jax.ShapeDtypeStruct((n, C_HEADS, dk, dv), F32)],
        scratch_shapes=[pltpu.VMEM((C_HEADS, dk, dv), F32)],
        compiler_params=_cparams("parallel", "arbitrary"), name="gla",
    )(proj, proj, proj, proj, w2, gb, s_init)


def _gated_headnorm(o, z, nw, heads, width):
    parts = []
    for h in range(heads):
        zz = z[:, h * width:(h + 1) * width]
        parts.append(_rms(o[:, h * width:(h + 1) * width], nw) * (zz * _sigmoid(zz)))
    return jnp.concatenate(parts, axis=1)


def _mix_even_kernel(oa_ref, ob_ref, z_ref, nw_ref, w_ref, r_ref, g_ref, out_ref):
    ka = A_HEADS * HEAD_DIM
    obn = _gated_headnorm(ob_ref[...], z_ref[...], nw_ref[...], B_HEADS, B_DK)
    m = _dot(oa_ref[...], w_ref[0:ka, :]) + _dot(obn, w_ref[ka:, :])
    out_ref[...] = r_ref[...] + _rms(m, g_ref[...])


def _mix_even(o_a, o_b, proj, nw, w_out, r, g_post, tm):
    m, d = r.shape
    ka = o_a.shape[1]
    kb = o_b.shape[1]
    z_block = (2 * 3 * A_HEADS * HEAD_DIM) // kb
    return pl.pallas_call(
        _mix_even_kernel, grid=(m // tm,),
        in_specs=[pl.BlockSpec((tm, ka), lambda i: (i, 0)),
                  pl.BlockSpec((tm, kb), lambda i: (i, 0)),
                  pl.BlockSpec((tm, kb), lambda i: (i, z_block)),
                  pl.BlockSpec((1, B_DK), lambda i: (0, 0)),
                  pl.BlockSpec((ka + kb, d), lambda i: (0, 0)),
                  pl.BlockSpec((tm, d), lambda i: (i, 0)),
                  pl.BlockSpec((1, d), lambda i: (0, 0))],
        out_specs=pl.BlockSpec((tm, d), lambda i: (i, 0)),
        out_shape=jax.ShapeDtypeStruct((m, d), F32),
        compiler_params=_cparams("parallel"), name="mix_even",
    )(o_a, o_b, proj, nw.reshape(1, -1), w_out, r, g_post.reshape(1, d))


def _mix_odd_kernel(o_ref, z_ref, nw_ref, w_ref, r_ref, g_ref, out_ref, *, dv):
    on = _gated_headnorm(o_ref[...], z_ref[...], nw_ref[...], C_HEADS, dv)
    out_ref[...] = r_ref[...] + _rms(_dot(on, w_ref[...]), g_ref[...])


def _mix_odd(o, proj, nw, w_out, r, g_post, tm):
    m, d = r.shape
    kv = o.shape[1]
    return pl.pallas_call(
        functools.partial(_mix_odd_kernel, dv=kv // C_HEADS), grid=(m // tm,),
        in_specs=[pl.BlockSpec((tm, kv), lambda i: (i, 0)),
                  pl.BlockSpec((tm, kv), lambda i: (i, 2)),
                  pl.BlockSpec((1, kv // C_HEADS), lambda i: (0, 0)),
                  pl.BlockSpec((kv, d), lambda i: (0, 0)),
                  pl.BlockSpec((tm, d), lambda i: (i, 0)),
                  pl.BlockSpec((1, d), lambda i: (0, 0))],
        out_specs=pl.BlockSpec((tm, d), lambda i: (i, 0)),
        out_shape=jax.ShapeDtypeStruct((m, d), F32),
        compiler_params=_cparams("parallel"), name="mix_odd",
    )(o, proj, nw.reshape(1, -1), w_out, r, g_post.reshape(1, d))


def _ffn_kernel(x_ref, gpre_ref, wg_ref, wv_ref, cwg_ref, cwv_ref, wd_ref, gpost_ref, ig_ref, iv_ref,
                o_ref, tg_ref, tv_ref, h_ref, ug, uv, *carry, tm, halo, shift):
    i = pl.program_id(1)
    j = pl.program_id(2)

    @pl.when(j == 0)
    def _():
        h_ref[...] = _rms(x_ref[0], gpre_ref[...]).astype(BF16)
        o_ref[0] = jnp.zeros_like(o_ref[0])

    if carry:
        cg, cv = carry

        @pl.when(i == 0)
        def _():
            ug[0:halo, :] = ig_ref[0]
            uv[0:halo, :] = iv_ref[0]

        @pl.when(i > 0)
        def _():
            ug[0:halo, :] = cg[j]
            uv[0:halo, :] = cv[j]
    else:
        ug[0:halo, :] = ig_ref[0]
        uv[0:halo, :] = iv_ref[0]

    tf = ug.shape[1]
    sub = min(FFN_SUB, tf)
    cols = [slice(s * sub, (s + 1) * sub) for s in range(tf // sub)]
    for cs in cols:
        ug[halo:halo + tm, cs] = jnp.dot(h_ref[...], wg_ref[:, cs], preferred_element_type=F32)
        uv[halo:halo + tm, cs] = jnp.dot(h_ref[...], wv_ref[:, cs], preferred_element_type=F32)

    def conv(u, cw_ref, cs):
        y = cw_ref[0:1, cs] * u[pl.ds(halo - (FFN_CONV - 1) * shift, tm), cs]
        for tap in range(1, FFN_CONV):
            y = y + cw_ref[tap:tap + 1, cs] * u[pl.ds(halo - (FFN_CONV - 1 - tap) * shift, tm), cs]
        return y

    acc = None
    for cs in cols:
        act = _gelu_tanh(conv(ug, cwg_ref, cs)) * conv(uv, cwv_ref, cs)
        part = _dot(act, wd_ref[cs, :])
        acc = part if acc is None else acc + part
    tail_g = ug[tm:tm + halo, :]
    tail_v = uv[tm:tm + halo, :]
    tg_ref[0, 0] = tail_g
    tv_ref[0, 0] = tail_v
    if carry:
        cg[j] = tail_g
        cv[j] = tail_v
    o_ref[0] += acc

    @pl.when(j == pl.num_programs(2) - 1)
    def _():
        o_ref[0] = x_ref[0] + _rms(o_ref[0], gpost_ref[...])


def _ffn(x, g_pre, w_up, conv_w, w_down, g_post, init, layer, tm, tf, shift):
    n, t, d = x.shape
    dff = w_down.shape[1]
    halo = init.shape[1]
    ni, nj = t // tm, dff // tf
    scratch = [pltpu.VMEM((tm, d), BF16), pltpu.VMEM((halo + tm, tf), F32), pltpu.VMEM((halo + tm, tf), F32)]
    if ni > 1:
        scratch += [pltpu.VMEM((nj, halo, tf), F32), pltpu.VMEM((nj, halo, tf), F32)]
    out, tail_g, tail_v = pl.pallas_call(
        functools.partial(_ffn_kernel, tm=tm, halo=halo, shift=shift),
        grid=(n, ni, nj),
        in_specs=[pl.BlockSpec((1, tm, d), lambda b, i, j: (b, i, 0), pipeline_mode=pl.Buffered(1)),
                  pl.BlockSpec((1, d), lambda b, i, j: (0, 0)),
                  pl.BlockSpec((None, d, tf), lambda b, i, j: (layer, 0, j)),
                  pl.BlockSpec((None, d, tf), lambda b, i, j: (layer, 0, j + nj)),
                  pl.BlockSpec((None, FFN_CONV, tf), lambda b, i, j: (layer, 0, j)),
                  pl.BlockSpec((None, FFN_CONV, tf), lambda b, i, j: (layer, 0, j + nj)),
                  pl.BlockSpec((None, tf, d), lambda b, i, j: (layer, j, 0)),
                  pl.BlockSpec((1, d), lambda b, i, j: (0, 0)),
                  pl.BlockSpec((1, halo, tf), lambda b, i, j: (b, 0, j)),
                  pl.BlockSpec((1, halo, tf), lambda b, i, j: (b, 0, j + nj))],
        out_specs=[pl.BlockSpec((1, tm, d), lambda b, i, j: (b, i, 0), pipeline_mode=pl.Buffered(1)),
                   pl.BlockSpec((1, 1, halo, tf), lambda b, i, j: (b, i, 0, j)),
                   pl.BlockSpec((1, 1, halo, tf), lambda b, i, j: (b, i, 0, j))],
        out_shape=[jax.ShapeDtypeStruct((n, t, d), F32),
                   jax.ShapeDtypeStruct((n, ni, halo, dff), F32),
                   jax.ShapeDtypeStruct((n, ni, halo, dff), F32)],
        scratch_shapes=scratch,
        compiler_params=_cparams("parallel", "arbitrary", "arbitrary"), name="conv_ffn",
    )(x, g_pre.reshape(1, d), w_up, w_up, conv_w, conv_w, w_down, g_post.reshape(1, d), init, init)
    return out, jnp.concatenate([tail_g[:, ni - 1], tail_v[:, ni - 1]], axis=-1)


def _ple_kernel(r_ref, p_ref, wp_ref, wg_ref, o_ref):
    r = r_ref[...]
    o_ref[...] = r + _dot(p_ref[...], wp_ref[...]) * _sigmoid(_dot(r, wg_ref[...]))


def _ple(r, p, w_proj, w_gate, layer, tm):
    m, d = r.shape
    pd = p.shape[1]
    return pl.pallas_call(
        _ple_kernel, grid=(m // tm,),
        in_specs=[pl.BlockSpec((tm, d), lambda i: (i, 0)),
                  pl.BlockSpec((tm, pd), lambda i: (i, 0)),
                  pl.BlockSpec((None, pd, d), lambda i: (layer, 0, 0)),
                  pl.BlockSpec((None, d, d), lambda i: (layer, 0, 0))],
        out_specs=pl.BlockSpec((tm, d), lambda i: (i, 0)),
        out_shape=jax.ShapeDtypeStruct((m, d), F32),
        compiler_params=_cparams("parallel"), name="ple",
    )(r, p, w_proj, w_gate)


def _pad_cols(w, width):
    return jnp.pad(w, ((0, 0), (0, width - w.shape[1])))


def _prep_weights(w_in_even, w_out_even, w_in_odd, w_out_odd, gla_gate_w2, ffn_w_up, ffn_w_down,
                  ple_w_proj, ple_w_gate, dn_a_log, dn_dt_bias):
    a_qkv = 3 * A_HEADS * HEAD_DIM
    b_qkv = 3 * B_HEADS * B_DK
    ab = a_qkv + b_qkv
    nb = 2 * B_HEADS
    w0 = jnp.concatenate([w_in_even[0][:, :ab], w_in_even[0][:, ab + nb:], w_in_even[0][:, ab:ab + nb]], axis=1)
    w0 = _pad_cols(w0, -(-w0.shape[1] // PROJ_TN) * PROJ_TN).astype(BF16)
    d = w_in_odd.shape[1]
    qkv = 2 * (d // 2) + d
    w1 = jnp.concatenate([w_in_odd[0][:, :qkv], w_in_odd[0][:, qkv + C_GATE_RANK:],
                          w_in_odd[0][:, qkv:qkv + C_GATE_RANK]], axis=1)
    w1 = _pad_cols(w1, -(-w1.shape[1] // PROJ_TN) * PROJ_TN).astype(BF16)
    w2 = jnp.pad(gla_gate_w2[0], ((0, LANES - C_GATE_RANK), (0, 0))).astype(BF16)
    adt_row = jnp.pad(jnp.stack([dn_a_log[0], dn_dt_bias[0]]), ((0, 0), (B_HEADS, LANES - 2 * B_HEADS)))
    adt_col = jnp.stack([dn_a_log[0], dn_dt_bias[0]], axis=1)
    return dict(w0=w0, w1=w1, w2=w2, adt_row=adt_row, adt_col=adt_col,
                w_out_even=w_out_even[0].astype(BF16), w_out_odd=w_out_odd[0].astype(BF16),
                w_up=ffn_w_up.astype(BF16), w_down=ffn_w_down.astype(BF16),
                w_proj=ple_w_proj.astype(BF16), w_gate=ple_w_gate.astype(BF16))


def _layer_tail(r, p, layer, wt, prm, ffn_init, n, t, tm):
    d = r.shape[1]
    r, tail = _ffn(r.reshape(n, t, d), prm["norm_ffn_pre"][layer], wt["w_up"], prm["ffn_conv_w"],
                   wt["w_down"], prm["norm_ffn_post"][layer], ffn_init, layer, PROMPT_FFN_TM, PROMPT_FFN_TF, 1)
    r = _ple(r.reshape(n * t, d), p, wt["w_proj"], wt["w_gate"], layer, tm)
    return r, tail


def _prompt_group(x, p, wt, prm):
    n, t, d = x.shape
    m = n * t
    tm = 256
    dff2 = prm["ffn_conv_w"].shape[-1]
    r = x.reshape(m, d)
    proj = _rms_matmul(r, prm["norm_mix_pre"][0], wt["w0"], PROMPT_PROJ_TM, PROJ_TN)
    pv = proj.reshape(n, t, -1)
    o_a = _attn_prompt(pv, n, t).reshape(m, -1)
    width = 3 * B_HEADS * B_DK
    o_b, dn_state = _dn_prompt(pv, prm["dn_conv_w"][0], wt["adt_row"], jnp.zeros((n, 8, width), F32),
                               jnp.zeros((n, B_HEADS, B_DK, B_DK), F32))
    r = _mix_even(o_a, o_b.reshape(m, -1), proj, prm["dn_norm_w"][0], wt["w_out_even"], r,
                  prm["norm_mix_post"][0], tm)
    keep = min(A_WMAX, t)
    hd = A_HEADS * HEAD_DIM
    win_k = pv[:, t - keep:, hd:2 * hd].reshape(n, keep, A_HEADS, HEAD_DIM)
    win_v = pv[:, t - keep:, 2 * hd:3 * hd].reshape(n, keep, A_HEADS, HEAD_DIM)
    dn_conv = pv[:, t - (B_CONV - 1):, width:2 * width]
    ffn_zero = jnp.zeros((n, 8, dff2), F32)
    r, tail0 = _layer_tail(r, p[0].reshape(m, -1), 0, wt, prm, ffn_zero, n, t, tm)
    proj1 = _rms_matmul(r, prm["norm_mix_pre"][1], wt["w1"], PROMPT_PROJ_TM, PROJ_TN)
    o_c, gla_state = _gla(proj1.reshape(n, t, -1), wt["w2"], prm["gla_gate_b"][0].reshape(1, -1),
                          jnp.zeros((n, C_HEADS, d // (2 * C_HEADS), d // C_HEADS), F32))
    r = _mix_odd(o_c.reshape(m, -1), proj1, prm["gla_norm_w"][0], wt["w_out_odd"], r, prm["norm_mix_post"][1], tm)
    r, tail1 = _layer_tail(r, p[1].reshape(m, -1), 1, wt, prm, ffn_zero, n, t, tm)
    ffn_conv = jnp.stack([tail0[:, 8 - (FFN_CONV - 1):], tail1[:, 8 - (FFN_CONV - 1):]])
    return (r.reshape(n, t, d), win_k[None], win_v[None], dn_conv[None], dn_state[None], gla_state[None], ffn_conv)


def _sample_group(x, p, cache_k, cache_v, dn_conv_state, dn_state, gla_state, ffn_state, wt, prm):
    n, t, d = x.shape
    m = n * t
    tm = 256
    r = x.reshape(m, d)

    def time_major(a):
        return a.reshape(n, t, -1).transpose(1, 0, 2).reshape(1, m, -1)

    def seq_major(a):
        return a.reshape(t, n, -1).transpose(1, 0, 2).reshape(m, -1)

    def ffn_layer(r, layer):
        init = ffn_state[layer].transpose(1, 0, 2).reshape(1, (FFN_CONV - 1) * n, -1)
        rt, tail = _ffn(time_major(r), prm["norm_ffn_pre"][layer], wt["w_up"], prm["ffn_conv_w"],
                        wt["w_down"], prm["norm_ffn_post"][layer], init, layer, m, SAMPLE_FFN_TF, n)
        r = _ple(seq_major(rt), p[layer].reshape(m, -1), wt["w_proj"], wt["w_gate"], layer, tm)
        return r, tail.reshape(FFN_CONV - 1, n, -1).transpose(1, 0, 2)

    proj = _rms_matmul(r, prm["norm_mix_pre"][0], wt["w0"], m, PROJ_TN)
    pv = proj.reshape(n, t, -1)
    width = 3 * B_HEADS * B_DK
    heads = pv.reshape(n, t, -1, HEAD_DIM)
    o_a = _attn_sample(heads, cache_k[0], cache_v[0])
    ba_col = (2 * width + B_HEADS * B_DK) // LANES
    bat = heads[:, :, ba_col].transpose(0, 2, 1)
    o_b, dn_new = _dn_sample(heads, bat, prm["dn_conv_w"][0].reshape(B_CONV, -1, B_DK), wt["adt_col"],
                             dn_conv_state[0].reshape(n, B_CONV - 1, -1, B_DK),
                             dn_state[0].reshape(n, B_HEADS * B_DK, B_DK))
    r = _mix_even(o_a.reshape(m, -1), o_b.reshape(m, -1), proj, prm["dn_norm_w"][0], wt["w_out_even"], r,
                  prm["norm_mix_post"][0], tm)
    win_k = heads[:, :, A_HEADS:2 * A_HEADS]
    win_v = heads[:, :, 2 * A_HEADS:3 * A_HEADS]
    dn_conv = pv[:, t - (B_CONV - 1):, width:2 * width]
    r, tail0 = ffn_layer(r, 0)
    proj1 = _rms_matmul(r, prm["norm_mix_pre"][1], wt["w1"], m, PROJ_TN)
    o_c, gla_new = _gla(proj1.reshape(n, t, -1), wt["w2"], prm["gla_gate_b"][0].reshape(1, -1), gla_state[0])
    r = _mix_odd(o_c.reshape(m, -1), proj1, prm["gla_norm_w"][0], wt["w_out_odd"], r, prm["norm_mix_post"][1], tm)
    r, tail1 = ffn_layer(r, 1)
    return (r.reshape(n, t, d), win_k[None], win_v[None], dn_conv[None],
            dn_new.reshape(dn_state.shape), gla_new[None], jnp.stack([tail0, tail1]))


def kernel(x_prompt, x_sample, cache_win_k, cache_win_v, state_dn_conv, state_dn, state_gla, state_ffn_conv, p_prompt, p_sample, norm_mix_pre, norm_mix_post, norm_ffn_pre, norm_ffn_post, w_in_even, w_out_even, dn_conv_w, dn_a_log, dn_dt_bias, dn_norm_w, w_in_odd, gla_gate_w2, gla_gate_b, gla_norm_w, w_out_odd, ffn_w_up, ffn_conv_w, ffn_w_down, ple_w_proj, ple_w_gate):
    wt = _prep_weights(w_in_even, w_out_even, w_in_odd, w_out_odd, gla_gate_w2, ffn_w_up, ffn_w_down,
                       ple_w_proj, ple_w_gate, dn_a_log, dn_dt_bias)
    prm = dict(norm_mix_pre=norm_mix_pre, norm_mix_post=norm_mix_post, norm_ffn_pre=norm_ffn_pre,
               norm_ffn_post=norm_ffn_post, dn_conv_w=dn_conv_w, dn_norm_w=dn_norm_w, gla_gate_b=gla_gate_b,
               gla_norm_w=gla_norm_w, ffn_conv_w=ffn_conv_w)
    yp, kp, vp, dcp, dsp, gsp, fcp = _prompt_group(x_prompt, p_prompt, wt, prm)
    ys, ks, vs, dcs, dss, gss, fcs = _sample_group(x_sample, p_sample, cache_win_k, cache_win_v, state_dn_conv,
                                                   state_dn, state_gla, state_ffn_conv, wt, prm)
    return (yp, ys, kp, vp, dcp, dsp, gsp, fcp, ks, vs, dcs, dss, gss, fcs)
```

```python
import functools
import math

import jax
import jax.numpy as jnp
from jax import lax
from jax.experimental import pallas as pl
from jax.experimental.pallas import tpu as pltpu

F32 = jnp.float32
BF16 = jnp.bfloat16
EPS = 1e-6
NEG = -1e30
HIGHEST = lax.Precision.HIGHEST

LANES = 128
V7X_VMEM_LIMIT_BYTES = 56 * 2**20

HEAD_DIM = 128
A_HEADS = 8
A_KEYS = 128
A_DILATIONS = (1, 4, 16)
A_WMAX = 2048
B_HEADS = 8
B_DK = 128
B_CONV = 4
C_HEADS = 4
C_GATE_RANK = 16
C_GATE_TAU = 16.0
FFN_CONV = 3
DN_CHUNK = 64
GLA_CHUNK = 64
GLA_SUB = 16
GLA_SAFE_SPAN = 60.0
PROJ_TN = 1280
PROJ0_TN = 1536
BA_LANE = 128 - 2 * 8
PROMPT_PROJ_TM = 1024
PROMPT_FFN_TM = 1024
PROMPT_FFN_TF = 512
FFN_SUB = 512
SAMPLE_FFN_TF = 512
MIX_TM = 512
MIX_PARTS = 2
SAMPLE_GLA_SEQS = 2


def _cparams(*sem):
    return pltpu.CompilerParams(dimension_semantics=sem, vmem_limit_bytes=V7X_VMEM_LIMIT_BYTES)


def _dot(a, b):
    return jnp.dot(a.astype(BF16), b.astype(BF16), preferred_element_type=F32)


def _dot_nt(a, b):
    return lax.dot_general(a.astype(BF16), b.astype(BF16), (((1,), (1,)), ((), ())),
                           preferred_element_type=F32)


def _dot_tn(a, b):
    return lax.dot_general(a.astype(BF16), b.astype(BF16), (((0,), (0,)), ((), ())),
                           preferred_element_type=F32)


def _bmm(a, b):
    return lax.dot_general(a.astype(BF16), b.astype(BF16), (((2,), (1,)), ((0,), (0,))),
                           preferred_element_type=F32)


def _bmm_nt(a, b):
    return lax.dot_general(a.astype(BF16), b.astype(BF16), (((2,), (2,)), ((0,), (0,))),
                           preferred_element_type=F32)


def _bmm_tn(a, b):
    return lax.dot_general(a.astype(BF16), b.astype(BF16), (((1,), (1,)), ((0,), (0,))),
                           preferred_element_type=F32)


def _dot_f32(a, b):
    return jnp.dot(a, b, precision=HIGHEST, preferred_element_type=F32)


def _dot_tn_f32(a, b):
    return lax.dot_general(a, b, (((0,), (0,)), ((), ())), precision=HIGHEST,
                           preferred_element_type=F32)


def _rms(x, g):
    return x * lax.rsqrt(jnp.mean(x * x, axis=-1, keepdims=True) + EPS) * g


def _sigmoid(x):
    return 1.0 / (1.0 + jnp.exp(-x))


def _softplus(x):
    return jnp.maximum(x, 0.0) + jnp.log(1.0 + jnp.exp(-jnp.abs(x)))


def _gelu_tanh(x):
    return 0.5 * x * (1.0 + jnp.tanh(math.sqrt(2.0 / math.pi) * (x + 0.044715 * (x * x * x))))


def _rms_matmul_kernel(x_ref, g_ref, w_ref, o_ref, h_ref, *, lead):
    j = pl.program_id(1)

    @pl.when(j == 0)
    def _():
        h_ref[...] = _rms(x_ref[...], g_ref[...]).astype(BF16)

    acc = jnp.dot(h_ref[...], w_ref[...], preferred_element_type=F32)
    if lead:
        last = pl.num_programs(1) - 1

        @pl.when(j < last)
        def _():
            o_ref[...] = acc

        @pl.when(j == last)
        def _():
            o_ref[...] = pltpu.roll(acc, acc.shape[1] - lead, axis=1)
    else:
        o_ref[...] = acc


def _rms_matmul(x, g, w, tm, tn, lead=0):
    m, k = x.shape
    n = w.shape[1]
    return pl.pallas_call(
        functools.partial(_rms_matmul_kernel, lead=lead),
        grid=(m // tm, n // tn),
        in_specs=[pl.BlockSpec((tm, k), lambda i, j: (i, 0)),
                  pl.BlockSpec((1, k), lambda i, j: (0, 0)),
                  pl.BlockSpec((k, tn), lambda i, j: (0, j))],
        out_specs=pl.BlockSpec((tm, tn), lambda i, j: (i, j)),
        out_shape=jax.ShapeDtypeStruct((m, n), F32),
        scratch_shapes=[pltpu.VMEM((tm, k), BF16)],
        compiler_params=_cparams("parallel", "arbitrary"),
        name="rms_matmul",
    )(x, g.reshape(1, k), w)


ATTN_SB = A_WMAX


def _attn_prompt_kernel(q_ref, kc_ref, kp_ref, vc_ref, vp_ref, o_ref, o_s, l_s, *, scale):
    has_prev = pl.program_id(2) > 0
    row = lax.broadcasted_iota(jnp.int32, (A_KEYS, 2 * A_KEYS), 0)
    col = lax.broadcasted_iota(jnp.int32, (A_KEYS, 2 * A_KEYS), 1)
    cur_ok = (col >= A_KEYS) & (col - A_KEYS <= row)
    prev_ok = (col < A_KEYS) & (col >= row)

    def tile(ref, start, d):
        if d == 1:
            return ref[0, pl.ds(start, A_KEYS), :]
        return ref[0, pl.ds(start, A_KEYS, stride=d), :]

    for g, d in enumerate(A_DILATIONS):
        span = A_KEYS * d
        for r in range(d):
            for j in range(ATTN_SB // span):
                start = r + span * j
                q = tile(q_ref, start, d) * scale
                if j > 0:
                    kp, vp = tile(kc_ref, start - span, d), tile(vc_ref, start - span, d)
                    ok = cur_ok | prev_ok
                else:
                    kp, vp = tile(kp_ref, ATTN_SB - span + r, d), tile(vp_ref, ATTN_SB - span + r, d)
                    ok = cur_ok | (prev_ok & has_prev)
                s = _dot_nt(q, jnp.concatenate([kp, tile(kc_ref, start, d)], axis=0))
                s = jnp.where(ok, s, NEG)
                m = jnp.max(s, axis=-1, keepdims=True)
                p = jnp.exp(s - m)
                l = jnp.sum(p, axis=-1, keepdims=True)
                o = _dot(p, jnp.concatenate([vp, tile(vc_ref, start, d)], axis=0)) / l
                lse = jnp.broadcast_to(m + jnp.log(l), (A_KEYS, HEAD_DIM))
                if d == 1:
                    o_s[g, pl.ds(start, A_KEYS), :] = o
                    l_s[g, pl.ds(start, A_KEYS), :] = lse
                else:
                    o_s[g, pl.ds(start, A_KEYS, stride=d), :] = o
                    l_s[g, pl.ds(start, A_KEYS, stride=d), :] = lse
    la, lb, lc = l_s[0], l_s[1], l_s[2]
    m = jnp.maximum(jnp.maximum(la, lb), lc)
    wa, wb, wc = jnp.exp(la - m), jnp.exp(lb - m), jnp.exp(lc - m)
    o_ref[0] = (wa * o_s[0] + wb * o_s[1] + wc * o_s[2]) / (wa + wb + wc)


def _attn_prompt(proj, n, t):
    assert t % ATTN_SB == 0 and len(A_DILATIONS) == 3
    cur = lambda off: (lambda b, h, i: (b, i, off + h))
    prev = lambda off: (lambda b, h, i: (b, jnp.maximum(i - 1, 0), off + h))
    blk = (1, ATTN_SB, HEAD_DIM)
    return pl.pallas_call(
        functools.partial(_attn_prompt_kernel, scale=HEAD_DIM ** -0.5),
        grid=(n, A_HEADS, t // ATTN_SB),
        in_specs=[pl.BlockSpec(blk, cur(0)),
                  pl.BlockSpec(blk, cur(A_HEADS)), pl.BlockSpec(blk, prev(A_HEADS)),
                  pl.BlockSpec(blk, cur(2 * A_HEADS)), pl.BlockSpec(blk, prev(2 * A_HEADS))],
        out_specs=pl.BlockSpec(blk, lambda b, h, i: (b, i, h)),
        out_shape=jax.ShapeDtypeStruct((n, t, A_HEADS * HEAD_DIM), F32),
        scratch_shapes=[pltpu.VMEM((3, ATTN_SB, HEAD_DIM), F32), pltpu.VMEM((3, ATTN_SB, HEAD_DIM), F32)],
        compiler_params=_cparams("parallel", "parallel", "arbitrary"), name="attn_prompt",
    )(proj, proj, proj, proj, proj)


def _attn_sample_kernel(x_ref, k1_ref, k2_ref, v1_ref, v2_ref, o_ref, *, scale):
    tn = 4
    nq = tn * A_HEADS
    nk = A_KEYS * A_HEADS
    x = x_ref[0]
    q = (x[:, 0:A_HEADS] * scale).reshape(nq, HEAD_DIM)
    knew = x[:, A_HEADS:2 * A_HEADS].reshape(nq, HEAD_DIM)
    vnew = x[:, 2 * A_HEADS:3 * A_HEADS].reshape(nq, HEAD_DIM)

    def grid_masks(cols):
        qrow = lax.broadcasted_iota(jnp.int32, (nq, cols), 0)
        kcol = lax.broadcasted_iota(jnp.int32, (nq, cols), 1)
        return (qrow % A_HEADS) == (kcol % A_HEADS), qrow // A_HEADS, kcol // A_HEADS

    same_head, qt, key = grid_masks(nk)
    flat = lambda a: a.reshape(nk, HEAD_DIM)
    dense = slice(A_KEYS - A_KEYS // 4, A_KEYS)
    segs = [(flat(k1_ref[0, dense]), flat(v1_ref[0, dense]), jnp.where(same_head & (key >= qt), 1.0, 0.0))]
    for kr, vr in ((k1_ref, v1_ref), (k2_ref, v2_ref)):
        for t in range(tn):
            segs.append((flat(kr[0, :, t]), flat(vr[0, :, t]), jnp.where(same_head & (qt == t), 1.0, 0.0)))
    same_head, qt, kt = grid_masks(nq)
    segs.append((knew, vnew, jnp.where(same_head & (kt == qt), float(len(A_DILATIONS)),
                                       jnp.where(same_head & (kt < qt), 1.0, 0.0))))

    scores = [_dot_nt(q, k) for k, _, _ in segs]
    m = None
    for s, (_, _, mult) in zip(scores, segs):
        ms = jnp.max(jnp.where(mult > 0.0, s, NEG), axis=-1, keepdims=True)
        m = ms if m is None else jnp.maximum(m, ms)
    probs = [jnp.where(mult > 0.0, jnp.exp(s - m), 0.0) * mult for s, (_, _, mult) in zip(scores, segs)]
    l = sum(jnp.sum(p, axis=-1, keepdims=True) for p in probs)
    acc = sum(_dot(p, v) for p, (_, v, _) in zip(probs, segs))
    o_ref[0] = (acc / l).reshape(tn, A_HEADS, HEAD_DIM)


def _attn_sample(x4, cache_k, cache_v):
    n, tn = x4.shape[0], x4.shape[1]
    w = cache_k.shape[1]
    assert w == A_WMAX and tn == 4
    views = []
    for cache in (cache_k, cache_v):
        views += [cache.reshape(n, w // 4, 4, A_HEADS, HEAD_DIM), cache.reshape(n, w // 16, 16, A_HEADS, HEAD_DIM)]
    k1, k2, v1, v2 = views
    s1 = pl.BlockSpec((1, A_KEYS, 4, A_HEADS, HEAD_DIM), lambda b: (b, w // 4 // A_KEYS - 1, 0, 0, 0))
    s2 = pl.BlockSpec((1, A_KEYS, 4, A_HEADS, HEAD_DIM), lambda b: (b, 0, 0, 0, 0))
    return pl.pallas_call(
        functools.partial(_attn_sample_kernel, scale=HEAD_DIM ** -0.5),
        grid=(n,),
        in_specs=[pl.BlockSpec((1, tn, 3 * A_HEADS, HEAD_DIM), lambda b: (b, 0, 0, 0)), s1, s2, s1, s2],
        out_specs=pl.BlockSpec((1, tn, A_HEADS, HEAD_DIM), lambda b: (b, 0, 0, 0)),
        out_shape=jax.ShapeDtypeStruct((n, tn, A_HEADS, HEAD_DIM), F32),
        compiler_params=_cparams("parallel"), name="attn_sample",
    )(x4, k1, k2, v1, v2)


def _unit_lower_inverse_minus_identity(low):
    c = low.shape[-1]
    base = min(16, c)
    row = lax.broadcasted_iota(jnp.int32, (1, c, c), 1)
    col = lax.broadcasted_iota(jnp.int32, (1, c, c), 2)
    nil = jnp.where((row // base) == (col // base), -low, 0.0)
    q = nil
    pw = nil
    for _ in range(int(math.log2(base)) - 1):
        pw = _bmm(pw, pw)
        q = q + pw + _bmm(q, pw)
    b = base
    while b < c:
        sib = ((row // (2 * b)) == (col // (2 * b))) & ((row // b) != (col // b))
        off = jnp.where(sib, low, 0.0)
        t = off + _bmm(q, off)
        q = q - t - _bmm(t, q)
        b *= 2
    return q


def _dn_prompt_kernel(x_ref, ba_ref, cw_ref, adt_ref, cinit_ref, sinit_ref, o_ref, sout_ref,
                      xbuf, s_ref, *, c, nb):
    ci = pl.program_id(0)
    halo = 8
    hk = B_HEADS * B_DK
    pairs = [(b, h) for b in range(nb) for h in range(B_HEADS)]

    @pl.when(ci == 0)
    def _():
        xbuf[:, 0:halo, :] = cinit_ref[...]
        s_ref[...] = sinit_ref[...].reshape(s_ref.shape)

    xbuf[:, halo:halo + c, :] = x_ref[...]
    cw = cw_ref[...]
    y = cw[0:1, :] * xbuf[:, pl.ds(halo - B_CONV + 1, c), :]
    for i in range(1, B_CONV):
        y = y + cw[i:i + 1, :] * xbuf[:, pl.ds(halo - B_CONV + 1 + i, c), :]
    xbuf[:, 0:halo, :] = xbuf[:, c:c + halo, :]
    cq = y * _sigmoid(y)

    ba = ba_ref[...]
    beta_all = _sigmoid(ba)
    g_all = -jnp.exp(adt_ref[0:1, :]) * _softplus(ba + adt_ref[1:2, :])
    row = lax.broadcasted_iota(jnp.int32, (c, c), 0)
    col = lax.broadcasted_iota(jnp.int32, (c, c), 1)
    tri_l = (row >= col).astype(F32)
    tri_u = (row <= col).astype(F32)
    gcum_col = [_dot_f32(tri_l, g_all[b]) for b in range(nb)]
    gcum_row = [_dot_tn_f32(g_all[b], tri_u) for b in range(nb)]

    def heads(off):
        return jnp.stack([cq[b, :, off + h * B_DK:off + (h + 1) * B_DK] for b, h in pairs])

    q, k, v = heads(0), heads(hk), heads(2 * hk)
    q = q * lax.rsqrt(jnp.sum(q * q, axis=-1, keepdims=True) + EPS) * (B_DK ** -0.5)
    k = k * lax.rsqrt(jnp.sum(k * k, axis=-1, keepdims=True) + EPS)
    bl, al = BA_LANE, BA_LANE + B_HEADS
    beta = jnp.stack([beta_all[b, :, bl + h:bl + h + 1] for b, h in pairs])
    gc = jnp.stack([gcum_col[b][:, al + h:al + h + 1] for b, h in pairs])
    gr = jnp.stack([gcum_row[b][al + h:al + h + 1, :] for b, h in pairs])
    decay = jnp.exp(jnp.where((row >= col)[None], gc - gr, NEG))
    kb = k * beta
    low = jnp.where((row > col)[None], _bmm_nt(kb, k) * decay, 0.0)
    qinv = _unit_lower_inverse_minus_identity(low)
    eg = jnp.exp(gc)
    rhs = jnp.concatenate([v * beta, kb * eg], axis=2)
    sol = rhs + _bmm(qinv, rhs)
    u = sol[:, :, 0:B_DK]
    w = sol[:, :, B_DK:2 * B_DK]
    attn = _bmm_nt(q, k) * decay
    glast = gc[:, c - 1:c, :]
    kd = k * jnp.exp(glast - gc)
    s = s_ref[...]
    v_new = u - _bmm(w, s)
    o = _bmm(q * eg, s) + _bmm(attn, v_new)
    s_ref[...] = s * jnp.exp(glast) + _bmm_tn(kd, v_new)
    for b in range(nb):
        o_ref[b] = jnp.concatenate([o[b * B_HEADS + h] for h in range(B_HEADS)], axis=1)

    @pl.when(ci == pl.num_programs(0) - 1)
    def _():
        sout_ref[...] = s_ref[...].reshape(sout_ref.shape)


def _dn_prompt(proj, conv_w, adt, conv_init, s_init):
    n, t, _ = proj.shape
    c = math.gcd(t, DN_CHUNK)
    width = 3 * B_HEADS * B_DK
    return pl.pallas_call(
        functools.partial(_dn_prompt_kernel, c=c, nb=n),
        grid=(t // c,),
        in_specs=[pl.BlockSpec((n, c, width), lambda i: (0, i, 1)),
                  pl.BlockSpec((n, c, LANES), lambda i: (0, i, proj.shape[2] // LANES - 1)),
                  pl.BlockSpec((B_CONV, width), lambda i: (0, 0)),
                  pl.BlockSpec((2, LANES), lambda i: (0, 0)),
                  pl.BlockSpec((n, 8, width), lambda i: (0, 0, 0)),
                  pl.BlockSpec((n, B_HEADS, B_DK, B_DK), lambda i: (0, 0, 0, 0))],
        out_specs=[pl.BlockSpec((n, c, B_HEADS * B_DK), lambda i: (0, i, 0)),
                   pl.BlockSpec((n, B_HEADS, B_DK, B_DK), lambda i: (0, 0, 0, 0))],
        out_shape=[jax.ShapeDtypeStruct((n, t, B_HEADS * B_DK), F32),
                   jax.ShapeDtypeStruct((n, B_HEADS, B_DK, B_DK), F32)],
        scratch_shapes=[pltpu.VMEM((n, 8 + c, width), F32), pltpu.VMEM((n * B_HEADS, B_DK, B_DK), F32)],
        compiler_params=_cparams("arbitrary"), name="deltanet_prompt",
    )(proj, proj, conv_w, adt, conv_init, s_init)


DN_SAMPLE_SEQS = 4


def _dn_sample_kernel(x_ref, bat_ref, cw_ref, adt_ref, cs_ref, sinit_ref, o_ref, s_ref):
    tn = 4
    hk = B_HEADS * B_DK
    seqs = range(DN_SAMPLE_SEQS)
    xp = jnp.concatenate([cs_ref[...], x_ref[...]], axis=1)
    cw = cw_ref[...]
    y = cw[0] * xp[:, 0:tn]
    for i in range(1, B_CONV):
        y = y + cw[i] * xp[:, i:i + tn]
    cq = y * _sigmoid(y)
    q = cq[:, :, 0:B_HEADS]
    k = cq[:, :, B_HEADS:2 * B_HEADS]
    v = cq[:, :, 2 * B_HEADS:3 * B_HEADS]
    q = q * lax.rsqrt(jnp.sum(q * q, axis=-1, keepdims=True) + EPS) * (B_DK ** -0.5)
    k = k * lax.rsqrt(jnp.sum(k * k, axis=-1, keepdims=True) + EPS)
    bat = bat_ref[...]
    beta = _sigmoid(bat[:, BA_LANE:BA_LANE + B_HEADS, :])
    a = jnp.exp(-jnp.exp(adt_ref[:, 0:1]) * _softplus(bat[:, BA_LANE + B_HEADS:LANES, :] + adt_ref[:, 1:2]))
    lane_head = lax.broadcasted_iota(jnp.int32, (B_HEADS, hk), 1) // B_DK
    head_mask = lane_head == lax.broadcasted_iota(jnp.int32, (B_HEADS, hk), 0)

    def block_diag(x):
        return jnp.where(head_mask, jnp.concatenate([x] * B_HEADS, axis=1), 0.0)

    s_ref[...] = sinit_ref[...]
    for t in range(tn):
        kbd = [block_diag(k[b, t]) for b in seqs]
        ks = [_dot(kbd[b], s_ref[b]) for b in seqs]
        w = [beta[b, :, t:t + 1] * (v[b, t] - a[b, :, t:t + 1] * ks[b]) for b in seqs]
        upd = [_dot_tn(kbd[b], w[b]) for b in seqs]
        for b in seqs:
            for h in range(B_HEADS):
                rows = slice(h * B_DK, (h + 1) * B_DK)
                s_ref[b, rows, :] = s_ref[b, rows, :] * a[b, h:h + 1, t:t + 1] + upd[b][rows, :]
        for b in seqs:
            o_ref[b, t] = _dot(block_diag(q[b, t]), s_ref[b])


def _dn_sample(x4, bat, conv_w, adt_col, conv_state, s_init):
    n = x4.shape[0]
    nb = DN_SAMPLE_SEQS
    rows = 3 * B_HEADS
    hk = B_HEADS * B_DK
    return pl.pallas_call(
        _dn_sample_kernel,
        grid=(n // nb,),
        in_specs=[pl.BlockSpec((nb, 4, rows, B_DK), lambda b: (b, 0, 1, 0)),
                  pl.BlockSpec((nb, LANES, 4), lambda b: (b, 0, 0)),
                  pl.BlockSpec((B_CONV, rows, B_DK), lambda b: (0, 0, 0)),
                  pl.BlockSpec((B_HEADS, 2), lambda b: (0, 0)),
                  pl.BlockSpec((nb, B_CONV - 1, rows, B_DK), lambda b: (b, 0, 0, 0)),
                  pl.BlockSpec((nb, hk, B_DK), lambda b: (b, 0, 0))],
        out_specs=[pl.BlockSpec((nb, 4, B_HEADS, B_DK), lambda b: (b, 0, 0, 0)),
                   pl.BlockSpec((nb, hk, B_DK), lambda b: (b, 0, 0))],
        out_shape=[jax.ShapeDtypeStruct((n, 4, B_HEADS, B_DK), F32),
                   jax.ShapeDtypeStruct((n, hk, B_DK), F32)],
        compiler_params=_cparams("parallel"), name="deltanet_sample",
    )(x4, bat, conv_w, adt_col, conv_state, s_init)


def _gla_kernel(q_ref, k_ref, v_ref, glr_ref, w2_ref, gb_ref, sinit_ref, o_ref, sout_ref,
                s_ref, *, c_real, c, dk, dv, nb):
    ci = pl.program_id(1)
    sub = min(GLA_SUB, c)

    @pl.when(ci == 0)
    def _():
        s_ref[...] = sinit_ref[...]

    def load(ref, b):
        x = ref[b]
        if c_real < c:
            x = jnp.concatenate([x, jnp.zeros((c - c_real, x.shape[1]), F32)], axis=0)
        return x

    row = lax.broadcasted_iota(jnp.int32, (c, c), 0)
    col = lax.broadcasted_iota(jnp.int32, (c, c), 1)
    tri = (row >= col).astype(F32)
    row_valid = lax.broadcasted_iota(jnp.int32, (c, 1), 0) < c_real
    sub_row = lax.broadcasted_iota(jnp.int32, (sub, 1), 0)
    sub_col = lax.broadcasted_iota(jnp.int32, (sub, c), 1)
    key_row = lax.broadcasted_iota(jnp.int32, (c, 1), 0)
    ones = jnp.ones((c, LANES), F32)
    starts = range(0, c, sub)
    seqs = []
    for b in range(nb):
        pre = _dot(load(glr_ref, b), w2_ref[...]) + gb_ref[...]
        la = jnp.where(row_valid, (jnp.minimum(pre, 0.0) - jnp.log(1.0 + jnp.exp(-jnp.abs(pre)))) / C_GATE_TAU, 0.0)
        gc = _dot_f32(tri, la)
        seqs.append((la, gc, load(q_ref, b) * (dk ** -0.5), load(k_ref, b), load(v_ref, b)))
    spans = [gc[lo:lo + 1, :] - gc[lo + sub - 1:lo + sub, :] for _, gc, _, _, _ in seqs for lo in starts]
    safe = jnp.max(functools.reduce(jnp.maximum, spans)) <= GLA_SAFE_SPAN

    def head(b, h):
        la, gc, q, k, v = seqs[b]
        ks = slice(h * dk, (h + 1) * dk)
        return la[:, ks], gc[:, ks], q[:, ks], k[:, ks], v[:, h * dv:(h + 1) * dv]

    for b in range(nb):
        outs = []
        for h in range(C_HEADS):
            la, gc, q, k, v = head(b, h)
            rows = []
            for lo in starts:
                ref_pt = gc[lo:lo + 1, :]
                qsc = q[lo:lo + sub, :] * jnp.exp(gc[lo:lo + sub, :] - ref_pt)
                limit = jnp.where(safe, lo + sub, lo)
                ksc = jnp.where(key_row < limit, k * jnp.exp(jnp.minimum(ref_pt - gc, GLA_SAFE_SPAN)), 0.0)
                rows.append(_dot_nt(qsc, ksc))
            attn = jnp.where(row >= col, jnp.concatenate(rows, axis=0), 0.0)
            s = s_ref[b, h]
            o = _dot(attn, v) + _dot(q * jnp.exp(gc), s)
            glast = gc[c - 1:c, :]
            kd = k * jnp.exp(glast - gc)
            gl_col = jnp.exp(_dot_tn_f32(la, ones))[:, 0:1]
            s_ref[b, h] = s * gl_col + _dot_tn(kd, v)
            outs.append(o[0:c_real, :])
        o_ref[b] = jnp.concatenate(outs, axis=1)

    @pl.when(jnp.logical_not(safe))
    def _():
        for b in range(nb):
            fixes = []
            for h in range(C_HEADS):
                _, gc, q, k, v = head(b, h)
                rows = []
                for lo in starts:
                    qb = q[lo:lo + sub, :]
                    gblk = gc[lo:lo + sub, :]
                    blk = jnp.zeros((sub, c), F32)
                    for jj in range(sub):
                        j = lo + jj
                        e = jnp.exp(jnp.minimum(gblk - gc[j:j + 1, :], 0.0))
                        a = jnp.sum(qb * k[j:j + 1, :] * e, axis=-1, keepdims=True)
                        blk = jnp.where((sub_col == j) & (sub_row >= jj), a, blk)
                    rows.append(blk)
                fixes.append(_dot(jnp.concatenate(rows, axis=0), v)[0:c_real, :])
            o_ref[b] += jnp.concatenate(fixes, axis=1)

    @pl.when(ci == pl.num_programs(1) - 1)
    def _():
        sout_ref[...] = s_ref[...]


def _gla(proj, w2, gb, s_init, nb):
    n, t, _ = proj.shape
    dk, dv = s_init.shape[2], s_init.shape[3]
    c_real = math.gcd(t, GLA_CHUNK)
    c = max(c_real, 8)
    hk, hv = C_HEADS * dk, C_HEADS * dv
    return pl.pallas_call(
        functools.partial(_gla_kernel, c_real=c_real, c=c, dk=dk, dv=dv, nb=nb),
        grid=(n // nb, t // c_real),
        in_specs=[pl.BlockSpec((nb, c_real, hk), lambda b, i: (b, i, 0)),
                  pl.BlockSpec((nb, c_real, hk), lambda b, i: (b, i, 1)),
                  pl.BlockSpec((nb, c_real, hv), lambda b, i: (b, i, 2 * hk // hv)),
                  pl.BlockSpec((nb, c_real, LANES), lambda b, i: (b, i, (2 * hk + 2 * hv) // LANES)),
                  pl.BlockSpec((LANES, hk), lambda b, i: (0, 0)),
                  pl.BlockSpec((1, hk), lambda b, i: (0, 0)),
                  pl.BlockSpec((nb, C_HEADS, dk, dv), lambda b, i: (b, 0, 0, 0))],
        out_specs=[pl.BlockSpec((nb, c_real, hv), lambda b, i: (b, i, 0)),
                   pl.BlockSpec((nb, C_HEADS, dk, dv), lambda b, i: (b, 0, 0, 0))],
        out_shape=[jax.ShapeDtypeStruct((n, t, hv), F32),
                   jax.ShapeDtypeStruct((n, C_HEADS, dk, dv), F32)],
        scratch_shapes=[pltpu.VMEM((nb, C_HEADS, dk, dv), F32)],
        compiler_params=_cparams("parallel", "arbitrary"), name="gla",
    )(proj, proj, proj, proj, w2, gb, s_init)


def _gated_headnorm(o, z, nw, heads, width):
    parts = []
    for h in range(heads):
        zz = z[:, h * width:(h + 1) * width]
        parts.append(_rms(o[:, h * width:(h + 1) * width], nw) * (zz * _sigmoid(zz)))
    return jnp.concatenate(parts, axis=1)


def _row_parts(tm):
    tp = tm // MIX_PARTS if tm % (8 * MIX_PARTS) == 0 else tm
    return [slice(lo, lo + tp) for lo in range(0, tm, tp)]


def _mix_even_kernel(oa_ref, ob_ref, z_ref, nw_ref, w_ref, r_ref, g_ref, out_ref):
    ka = A_HEADS * HEAD_DIM
    for rows in _row_parts(out_ref.shape[0]):
        obn = _gated_headnorm(ob_ref[rows, :], z_ref[rows, :], nw_ref[...], B_HEADS, B_DK)
        m = _dot(oa_ref[rows, :], w_ref[0:ka, :]) + _dot(obn, w_ref[ka:, :])
        out_ref[rows, :] = r_ref[rows, :] + _rms(m, g_ref[...])


def _mix_even(o_a, o_b, proj, nw, w_out, r, g_post, tm):
    m, d = r.shape
    ka = o_a.shape[1]
    kb = o_b.shape[1]
    z_block = (2 * 3 * A_HEADS * HEAD_DIM) // kb
    return pl.pallas_call(
        _mix_even_kernel, grid=(m // tm,),
        in_specs=[pl.BlockSpec((tm, ka), lambda i: (i, 0)),
                  pl.BlockSpec((tm, kb), lambda i: (i, 0)),
                  pl.BlockSpec((tm, kb), lambda i: (i, z_block)),
                  pl.BlockSpec((1, B_DK), lambda i: (0, 0)),
                  pl.BlockSpec((ka + kb, d), lambda i: (0, 0)),
                  pl.BlockSpec((tm, d), lambda i: (i, 0)),
                  pl.BlockSpec((1, d), lambda i: (0, 0))],
        out_specs=pl.BlockSpec((tm, d), lambda i: (i, 0)),
        out_shape=jax.ShapeDtypeStruct((m, d), F32),
        compiler_params=_cparams("parallel"), name="mix_even",
    )(o_a, o_b, proj, nw.reshape(1, -1), w_out, r, g_post.reshape(1, d))


def _mix_odd_kernel(o_ref, z_ref, nw_ref, w_ref, r_ref, g_ref, out_ref, *, dv):
    for rows in _row_parts(out_ref.shape[0]):
        on = _gated_headnorm(o_ref[rows, :], z_ref[rows, :], nw_ref[...], C_HEADS, dv)
        out_ref[rows, :] = r_ref[rows, :] + _rms(_dot(on, w_ref[...]), g_ref[...])


def _mix_odd(o, proj, nw, w_out, r, g_post, tm):
    m, d = r.shape
    kv = o.shape[1]
    return pl.pallas_call(
        functools.partial(_mix_odd_kernel, dv=kv // C_HEADS), grid=(m // tm,),
        in_specs=[pl.BlockSpec((tm, kv), lambda i: (i, 0)),
                  pl.BlockSpec((tm, kv), lambda i: (i, 2)),
                  pl.BlockSpec((1, kv // C_HEADS), lambda i: (0, 0)),
                  pl.BlockSpec((kv, d), lambda i: (0, 0)),
                  pl.BlockSpec((tm, d), lambda i: (i, 0)),
                  pl.BlockSpec((1, d), lambda i: (0, 0))],
        out_specs=pl.BlockSpec((tm, d), lambda i: (i, 0)),
        out_shape=jax.ShapeDtypeStruct((m, d), F32),
        compiler_params=_cparams("parallel"), name="mix_odd",
    )(o, proj, nw.reshape(1, -1), w_out, r, g_post.reshape(1, d))


def _ffn_kernel(x_ref, gpre_ref, wg_ref, wv_ref, cwg_ref, cwv_ref, wd_ref, gpost_ref, ig_ref, iv_ref,
                o_ref, tg_ref, tv_ref, h_ref, ug, uv, *carry, tm, halo, shift):
    i = pl.program_id(1)
    j = pl.program_id(2)

    @pl.when(j == 0)
    def _():
        h_ref[...] = _rms(x_ref[0], gpre_ref[...]).astype(BF16)
        o_ref[0] = jnp.zeros_like(o_ref[0])

    if carry:
        cg, cv = carry

        @pl.when(i == 0)
        def _():
            ug[0:halo, :] = ig_ref[0]
            uv[0:halo, :] = iv_ref[0]

        @pl.when(i > 0)
        def _():
            ug[0:halo, :] = cg[j]
            uv[0:halo, :] = cv[j]
    else:
        ug[0:halo, :] = ig_ref[0]
        uv[0:halo, :] = iv_ref[0]

    tf = ug.shape[1]
    sub = min(FFN_SUB, tf)
    cols = [slice(s * sub, (s + 1) * sub) for s in range(tf // sub)]
    for cs in cols:
        ug[halo:halo + tm, cs] = jnp.dot(h_ref[...], wg_ref[:, cs], preferred_element_type=F32)
        uv[halo:halo + tm, cs] = jnp.dot(h_ref[...], wv_ref[:, cs], preferred_element_type=F32)

    def conv(u, cw_ref, cs):
        y = cw_ref[0:1, cs] * u[pl.ds(halo - (FFN_CONV - 1) * shift, tm), cs]
        for tap in range(1, FFN_CONV):
            y = y + cw_ref[tap:tap + 1, cs] * u[pl.ds(halo - (FFN_CONV - 1 - tap) * shift, tm), cs]
        return y

    acc = None
    for cs in cols:
        act = _gelu_tanh(conv(ug, cwg_ref, cs)) * conv(uv, cwv_ref, cs)
        part = _dot(act, wd_ref[cs, :])
        acc = part if acc is None else acc + part
    tail_g = ug[tm:tm + halo, :]
    tail_v = uv[tm:tm + halo, :]
    tg_ref[0, 0] = tail_g
    tv_ref[0, 0] = tail_v
    if carry:
        cg[j] = tail_g
        cv[j] = tail_v
    o_ref[0] += acc

    @pl.when(j == pl.num_programs(2) - 1)
    def _():
        o_ref[0] = x_ref[0] + _rms(o_ref[0], gpost_ref[...])


def _ffn(x, g_pre, w_up, conv_w, w_down, g_post, init, layer, tm, tf, shift):
    n, t, d = x.shape
    dff = w_down.shape[1]
    halo = init.shape[1]
    ni, nj = t // tm, dff // tf
    scratch = [pltpu.VMEM((tm, d), BF16), pltpu.VMEM((halo + tm, tf), F32), pltpu.VMEM((halo + tm, tf), F32)]
    if ni > 1:
        scratch += [pltpu.VMEM((nj, halo, tf), F32), pltpu.VMEM((nj, halo, tf), F32)]
    out, tail_g, tail_v = pl.pallas_call(
        functools.partial(_ffn_kernel, tm=tm, halo=halo, shift=shift),
        grid=(n, ni, nj),
        in_specs=[pl.BlockSpec((1, tm, d), lambda b, i, j: (b, i, 0), pipeline_mode=pl.Buffered(1)),
                  pl.BlockSpec((1, d), lambda b, i, j: (0, 0)),
                  pl.BlockSpec((None, d, tf), lambda b, i, j: (layer, 0, j)),
                  pl.BlockSpec((None, d, tf), lambda b, i, j: (layer, 0, j + nj)),
                  pl.BlockSpec((None, FFN_CONV, tf), lambda b, i, j: (layer, 0, j)),
                  pl.BlockSpec((None, FFN_CONV, tf), lambda b, i, j: (layer, 0, j + nj)),
                  pl.BlockSpec((None, tf, d), lambda b, i, j: (layer, j, 0)),
                  pl.BlockSpec((1, d), lambda b, i, j: (0, 0)),
                  pl.BlockSpec((1, halo, tf), lambda b, i, j: (b, 0, j)),
                  pl.BlockSpec((1, halo, tf), lambda b, i, j: (b, 0, j + nj))],
        out_specs=[pl.BlockSpec((1, tm, d), lambda b, i, j: (b, i, 0), pipeline_mode=pl.Buffered(1)),
                   pl.BlockSpec((1, 1, halo, tf), lambda b, i, j: (b, i, 0, j)),
                   pl.BlockSpec((1, 1, halo, tf), lambda b, i, j: (b, i, 0, j))],
        out_shape=[jax.ShapeDtypeStruct((n, t, d), F32),
                   jax.ShapeDtypeStruct((n, ni, halo, dff), F32),
                   jax.ShapeDtypeStruct((n, ni, halo, dff), F32)],
        scratch_shapes=scratch,
        compiler_params=_cparams("parallel", "arbitrary", "arbitrary"), name="conv_ffn",
    )(x, g_pre.reshape(1, d), w_up, w_up, conv_w, conv_w, w_down, g_post.reshape(1, d), init, init)
    return out, jnp.concatenate([tail_g[:, ni - 1], tail_v[:, ni - 1]], axis=-1)


def _ple_kernel(r_ref, p_ref, wp_ref, wg_ref, o_ref):
    r = r_ref[...]
    o_ref[...] = r + _dot(p_ref[...], wp_ref[...]) * _sigmoid(_dot(r, wg_ref[...]))


def _ple(r, p, w_proj, w_gate, layer, tm):
    m, d = r.shape
    pd = p.shape[1]
    return pl.pallas_call(
        _ple_kernel, grid=(m // tm,),
        in_specs=[pl.BlockSpec((tm, d), lambda i: (i, 0)),
                  pl.BlockSpec((tm, pd), lambda i: (i, 0)),
                  pl.BlockSpec((None, pd, d), lambda i: (layer, 0, 0)),
                  pl.BlockSpec((None, d, d), lambda i: (layer, 0, 0))],
        out_specs=pl.BlockSpec((tm, d), lambda i: (i, 0)),
        out_shape=jax.ShapeDtypeStruct((m, d), F32),
        compiler_params=_cparams("parallel"), name="ple",
    )(r, p, w_proj, w_gate)


def _reorder_cols(w, spans, tn):
    parts = [w[:, a:b].astype(BF16) for a, b in spans]
    width = sum(p.shape[1] for p in parts)
    parts.append(jnp.zeros((w.shape[0], -width % tn), BF16))
    return jnp.concatenate(parts, axis=1)


def _prep_weights(w_in_even, w_out_even, w_in_odd, w_out_odd, gla_gate_w2, ffn_w_up, ffn_w_down,
                  ple_w_proj, ple_w_gate, dn_a_log, dn_dt_bias):
    assert (3 * A_HEADS * HEAD_DIM + 3 * B_HEADS * B_DK) % PROJ0_TN == 0
    w0 = _reorder_cols(w_in_even[0], [(0, None)], PROJ0_TN)
    assert w0.shape[1] - w_in_even.shape[2] >= LANES - 2 * B_HEADS
    d = w_in_odd.shape[1]
    qkv = 2 * (d // 2) + d
    w1 = _reorder_cols(w_in_odd[0], [(0, qkv), (qkv + C_GATE_RANK, None), (qkv, qkv + C_GATE_RANK)], PROJ_TN)
    w2 = jnp.pad(gla_gate_w2[0], ((0, LANES - C_GATE_RANK), (0, 0))).astype(BF16)
    adt_row = jnp.pad(jnp.stack([dn_a_log[0], dn_dt_bias[0]]), ((0, 0), (BA_LANE + B_HEADS, 0)))
    adt_col = jnp.stack([dn_a_log[0], dn_dt_bias[0]], axis=1)
    return dict(w0=w0, w1=w1, w2=w2, adt_row=adt_row, adt_col=adt_col,
                w_out_even=w_out_even[0].astype(BF16), w_out_odd=w_out_odd[0].astype(BF16),
                w_up=ffn_w_up.astype(BF16), w_down=ffn_w_down.astype(BF16),
                w_proj=ple_w_proj.astype(BF16), w_gate=ple_w_gate.astype(BF16))


def _layer_tail(r, p, layer, wt, prm, ffn_init, n, t, tm):
    d = r.shape[1]
    r, tail = _ffn(r.reshape(n, t, d), prm["norm_ffn_pre"][layer], wt["w_up"], prm["ffn_conv_w"],
                   wt["w_down"], prm["norm_ffn_post"][layer], ffn_init, layer, PROMPT_FFN_TM, PROMPT_FFN_TF, 1)
    r = _ple(r.reshape(n * t, d), p, wt["w_proj"], wt["w_gate"], layer, tm)
    return r, tail


def _prompt_group(x, p, wt, prm):
    n, t, d = x.shape
    m = n * t
    tm = MIX_TM
    dff2 = prm["ffn_conv_w"].shape[-1]
    r = x.reshape(m, d)
    proj = _rms_matmul(r, prm["norm_mix_pre"][0], wt["w0"], PROMPT_PROJ_TM, PROJ0_TN, lead=2 * B_HEADS)
    pv = proj.reshape(n, t, -1)
    o_a = _attn_prompt(pv, n, t).reshape(m, -1)
    width = 3 * B_HEADS * B_DK
    o_b, dn_state = _dn_prompt(pv, prm["dn_conv_w"][0], wt["adt_row"], jnp.zeros((n, 8, width), F32),
                               jnp.zeros((n, B_HEADS, B_DK, B_DK), F32))
    r = _mix_even(o_a, o_b.reshape(m, -1), proj, prm["dn_norm_w"][0], wt["w_out_even"], r,
                  prm["norm_mix_post"][0], tm)
    keep = min(A_WMAX, t)
    hd = A_HEADS * HEAD_DIM
    win_k = pv[:, t - keep:, hd:2 * hd].reshape(n, keep, A_HEADS, HEAD_DIM)
    win_v = pv[:, t - keep:, 2 * hd:3 * hd].reshape(n, keep, A_HEADS, HEAD_DIM)
    dn_conv = pv[:, t - (B_CONV - 1):, width:2 * width]
    ffn_zero = jnp.zeros((n, 8, dff2), F32)
    r, tail0 = _layer_tail(r, p[0].reshape(m, -1), 0, wt, prm, ffn_zero, n, t, tm)
    proj1 = _rms_matmul(r, prm["norm_mix_pre"][1], wt["w1"], PROMPT_PROJ_TM, PROJ_TN)
    o_c, gla_state = _gla(proj1.reshape(n, t, -1), wt["w2"], prm["gla_gate_b"][0].reshape(1, -1),
                          jnp.zeros((n, C_HEADS, d // (2 * C_HEADS), d // C_HEADS), F32), n)
    r = _mix_odd(o_c.reshape(m, -1), proj1, prm["gla_norm_w"][0], wt["w_out_odd"], r, prm["norm_mix_post"][1], tm)
    r, tail1 = _layer_tail(r, p[1].reshape(m, -1), 1, wt, prm, ffn_zero, n, t, tm)
    ffn_conv = jnp.stack([tail0[:, 8 - (FFN_CONV - 1):], tail1[:, 8 - (FFN_CONV - 1):]])
    return (r.reshape(n, t, d), win_k[None], win_v[None], dn_conv[None], dn_state[None], gla_state[None], ffn_conv)


def _sample_group(x, p, cache_k, cache_v, dn_conv_state, dn_state, gla_state, ffn_state, wt, prm):
    n, t, d = x.shape
    m = n * t
    tm = MIX_TM
    r = x.reshape(m, d)

    def time_major(a):
        return a.reshape(n, t, -1).transpose(1, 0, 2).reshape(1, m, -1)

    def seq_major(a):
        return a.reshape(t, n, -1).transpose(1, 0, 2).reshape(m, -1)

    def ffn_layer(r, layer):
        init = ffn_state[layer].transpose(1, 0, 2).reshape(1, (FFN_CONV - 1) * n, -1)
        rt, tail = _ffn(time_major(r), prm["norm_ffn_pre"][layer], wt["w_up"], prm["ffn_conv_w"],
                        wt["w_down"], prm["norm_ffn_post"][layer], init, layer, m, SAMPLE_FFN_TF, n)
        r = _ple(seq_major(rt), p[layer].reshape(m, -1), wt["w_proj"], wt["w_gate"], layer, tm)
        return r, tail.reshape(FFN_CONV - 1, n, -1).transpose(1, 0, 2)

    proj = _rms_matmul(r, prm["norm_mix_pre"][0], wt["w0"], m, PROJ0_TN, lead=2 * B_HEADS)
    pv = proj.reshape(n, t, -1)
    width = 3 * B_HEADS * B_DK
    heads = pv.reshape(n, t, -1, HEAD_DIM)
    o_a = _attn_sample(heads, cache_k[0], cache_v[0])
    bat = heads[:, :, heads.shape[2] - 1].transpose(0, 2, 1)
    o_b, dn_new = _dn_sample(heads, bat, prm["dn_conv_w"][0].reshape(B_CONV, -1, B_DK), wt["adt_col"],
                             dn_conv_state[0].reshape(n, B_CONV - 1, -1, B_DK),
                             dn_state[0].reshape(n, B_HEADS * B_DK, B_DK))
    r = _mix_even(o_a.reshape(m, -1), o_b.reshape(m, -1), proj, prm["dn_norm_w"][0], wt["w_out_even"], r,
                  prm["norm_mix_post"][0], tm)
    win_k = heads[:, :, A_HEADS:2 * A_HEADS]
    win_v = heads[:, :, 2 * A_HEADS:3 * A_HEADS]
    dn_conv = pv[:, t - (B_CONV - 1):, width:2 * width]
    r, tail0 = ffn_layer(r, 0)
    proj1 = _rms_matmul(r, prm["norm_mix_pre"][1], wt["w1"], m, PROJ_TN)
    o_c, gla_new = _gla(proj1.reshape(n, t, -1), wt["w2"], prm["gla_gate_b"][0].reshape(1, -1), gla_state[0],
                        SAMPLE_GLA_SEQS)
    r = _mix_odd(o_c.reshape(m, -1), proj1, prm["gla_norm_w"][0], wt["w_out_odd"], r, prm["norm_mix_post"][1], tm)
    r, tail1 = ffn_layer(r, 1)
    return (r.reshape(n, t, d), win_k[None], win_v[None], dn_conv[None],
            dn_new.reshape(dn_state.shape), gla_new[None], jnp.stack([tail0, tail1]))


def kernel(x_prompt, x_sample, cache_win_k, cache_win_v, state_dn_conv, state_dn, state_gla, state_ffn_conv, p_prompt, p_sample, norm_mix_pre, norm_mix_post, norm_ffn_pre, norm_ffn_post, w_in_even, w_out_even, dn_conv_w, dn_a_log, dn_dt_bias, dn_norm_w, w_in_odd, gla_gate_w2, gla_gate_b, gla_norm_w, w_out_odd, ffn_w_up, ffn_conv_w, ffn_w_down, ple_w_proj, ple_w_gate):
    wt = _prep_weights(w_in_even, w_out_even, w_in_odd, w_out_odd, gla_gate_w2, ffn_w_up, ffn_w_down,
                       ple_w_proj, ple_w_gate, dn_a_log, dn_dt_bias)
    prm = dict(norm_mix_pre=norm_mix_pre, norm_mix_post=norm_mix_post, norm_ffn_pre=norm_ffn_pre,
               norm_ffn_post=norm_ffn_post, dn_conv_w=dn_conv_w, dn_norm_w=dn_norm_w, gla_gate_b=gla_gate_b,
               gla_norm_w=gla_norm_w, ffn_conv_w=ffn_conv_w)
    yp, kp, vp, dcp, dsp, gsp, fcp = _prompt_group(x_prompt, p_prompt, wt, prm)
    ys, ks, vs, dcs, dss, gss, fcs = _sample_group(x_sample, p_sample, cache_win_k, cache_win_v, state_dn_conv,
                                                   state_dn, state_gla, state_ffn_conv, wt, prm)
    return (yp, ys, kp, vp, dcp, dsp, gsp, fcp, ks, vs, dcs, dss, gss, fcs)
```

```python
import functools
import math

import jax
import jax.numpy as jnp
from jax import lax
from jax.experimental import pallas as pl
from jax.experimental.pallas import tpu as pltpu

F32 = jnp.float32
BF16 = jnp.bfloat16
EPS = 1e-6
NEG = -1e30
HIGHEST = lax.Precision.HIGHEST

LANES = 128
V7X_VMEM_LIMIT_BYTES = 60 * 2**20

HEAD_DIM = 128
A_HEADS = 8
A_KEYS = 128
A_DILATIONS = (1, 4, 16)
A_WMAX = 2048
B_HEADS = 8
B_DK = 128
B_CONV = 4
C_HEADS = 4
C_GATE_RANK = 16
C_GATE_TAU = 16.0
FFN_CONV = 3
DN_CHUNK = 64
GLA_CHUNK = 64
GLA_SUB = 16
GLA_SAFE_SPAN = 60.0
PROJ_TN = 1280
PROJ0_TN = 1536
BA_LANE = 128 - 2 * 8
PROMPT_PROJ_TM = 1024
PROMPT_FFN_TM = 1024
PROMPT_FFN_TF = 512
FFN_SUB = 512
SAMPLE_FFN_TF = 512
MIX_TM = 512
MIX_PARTS = 2
SAMPLE_GLA_SEQS = 4
SAMPLE_ATTN_SEQS = 2


def _cparams(*sem):
    return pltpu.CompilerParams(dimension_semantics=sem, vmem_limit_bytes=V7X_VMEM_LIMIT_BYTES)


def _dot(a, b):
    return jnp.dot(a.astype(BF16), b.astype(BF16), preferred_element_type=F32)


def _dot_nt(a, b):
    return lax.dot_general(a.astype(BF16), b.astype(BF16), (((1,), (1,)), ((), ())),
                           preferred_element_type=F32)


def _dot_tn(a, b):
    return lax.dot_general(a.astype(BF16), b.astype(BF16), (((0,), (0,)), ((), ())),
                           preferred_element_type=F32)


def _bmm(a, b):
    return lax.dot_general(a.astype(BF16), b.astype(BF16), (((2,), (1,)), ((0,), (0,))),
                           preferred_element_type=F32)


def _bmm_nt(a, b):
    return lax.dot_general(a.astype(BF16), b.astype(BF16), (((2,), (2,)), ((0,), (0,))),
                           preferred_element_type=F32)


def _bmm_tn(a, b):
    return lax.dot_general(a.astype(BF16), b.astype(BF16), (((1,), (1,)), ((0,), (0,))),
                           preferred_element_type=F32)


def _dot_f32(a, b):
    return jnp.dot(a, b, precision=HIGHEST, preferred_element_type=F32)


def _dot_tn_f32(a, b):
    return lax.dot_general(a, b, (((0,), (0,)), ((), ())), precision=HIGHEST,
                           preferred_element_type=F32)


def _rms(x, g):
    return x * lax.rsqrt(jnp.mean(x * x, axis=-1, keepdims=True) + EPS) * g


def _sigmoid(x):
    return 1.0 / (1.0 + jnp.exp(-x))


def _softplus(x):
    return jnp.maximum(x, 0.0) + jnp.log(1.0 + jnp.exp(-jnp.abs(x)))


def _gelu_tanh(x):
    return 0.5 * x * (1.0 + jnp.tanh(math.sqrt(2.0 / math.pi) * (x + 0.044715 * (x * x * x))))


def _rms_matmul_kernel(x_ref, g_ref, w_ref, o_ref, h_ref, *, lead):
    j = pl.program_id(1)

    @pl.when(j == 0)
    def _():
        h_ref[...] = _rms(x_ref[...], g_ref[...]).astype(BF16)

    acc = jnp.dot(h_ref[...], w_ref[...], preferred_element_type=F32)
    if lead:
        last = pl.num_programs(1) - 1

        @pl.when(j < last)
        def _():
            o_ref[...] = acc

        @pl.when(j == last)
        def _():
            o_ref[...] = pltpu.roll(acc, acc.shape[1] - lead, axis=1)
    else:
        o_ref[...] = acc


def _rms_matmul(x, g, w, tm, tn, lead=0):
    m, k = x.shape
    n = w.shape[1]
    return pl.pallas_call(
        functools.partial(_rms_matmul_kernel, lead=lead),
        grid=(m // tm, n // tn),
        in_specs=[pl.BlockSpec((tm, k), lambda i, j: (i, 0)),
                  pl.BlockSpec((1, k), lambda i, j: (0, 0)),
                  pl.BlockSpec((k, tn), lambda i, j: (0, j))],
        out_specs=pl.BlockSpec((tm, tn), lambda i, j: (i, j)),
        out_shape=jax.ShapeDtypeStruct((m, n), F32),
        scratch_shapes=[pltpu.VMEM((tm, k), BF16)],
        compiler_params=_cparams("parallel", "arbitrary"),
        name="rms_matmul",
    )(x, g.reshape(1, k), w)


ATTN_SB = A_WMAX


def _attn_prompt_kernel(q_ref, kc_ref, kp_ref, vc_ref, vp_ref, o_ref, o_s, l_s, *, scale):
    has_prev = pl.program_id(2) > 0
    row = lax.broadcasted_iota(jnp.int32, (A_KEYS, 2 * A_KEYS), 0)
    col = lax.broadcasted_iota(jnp.int32, (A_KEYS, 2 * A_KEYS), 1)
    cur_ok = (col >= A_KEYS) & (col - A_KEYS <= row)
    prev_ok = (col < A_KEYS) & (col >= row)

    def tile(ref, start, d):
        if d == 1:
            return ref[0, pl.ds(start, A_KEYS), :]
        return ref[0, pl.ds(start, A_KEYS, stride=d), :]

    for g, d in enumerate(A_DILATIONS):
        span = A_KEYS * d
        for r in range(d):
            for j in range(ATTN_SB // span):
                start = r + span * j
                q = tile(q_ref, start, d) * scale
                if j > 0:
                    kp, vp = tile(kc_ref, start - span, d), tile(vc_ref, start - span, d)
                    ok = cur_ok | prev_ok
                else:
                    kp, vp = tile(kp_ref, ATTN_SB - span + r, d), tile(vp_ref, ATTN_SB - span + r, d)
                    ok = cur_ok | (prev_ok & has_prev)
                s = _dot_nt(q, jnp.concatenate([kp, tile(kc_ref, start, d)], axis=0))
                s = jnp.where(ok, s, NEG)
                m = jnp.max(s, axis=-1, keepdims=True)
                p = jnp.exp(s - m)
                l = jnp.sum(p, axis=-1, keepdims=True)
                o = _dot(p, jnp.concatenate([vp, tile(vc_ref, start, d)], axis=0)) / l
                lse = jnp.broadcast_to(m + jnp.log(l), (A_KEYS, HEAD_DIM))
                if d == 1:
                    o_s[g, pl.ds(start, A_KEYS), :] = o
                    l_s[g, pl.ds(start, A_KEYS), :] = lse
                else:
                    o_s[g, pl.ds(start, A_KEYS, stride=d), :] = o
                    l_s[g, pl.ds(start, A_KEYS, stride=d), :] = lse
    la, lb, lc = l_s[0], l_s[1], l_s[2]
    m = jnp.maximum(jnp.maximum(la, lb), lc)
    wa, wb, wc = jnp.exp(la - m), jnp.exp(lb - m), jnp.exp(lc - m)
    o_ref[0] = (wa * o_s[0] + wb * o_s[1] + wc * o_s[2]) / (wa + wb + wc)


def _attn_prompt(proj, n, t):
    assert t % ATTN_SB == 0 and len(A_DILATIONS) == 3
    cur = lambda off: (lambda b, h, i: (b, i, off + h))
    prev = lambda off: (lambda b, h, i: (b, jnp.maximum(i - 1, 0), off + h))
    blk = (1, ATTN_SB, HEAD_DIM)
    return pl.pallas_call(
        functools.partial(_attn_prompt_kernel, scale=HEAD_DIM ** -0.5),
        grid=(n, A_HEADS, t // ATTN_SB),
        in_specs=[pl.BlockSpec(blk, cur(0)),
                  pl.BlockSpec(blk, cur(A_HEADS)), pl.BlockSpec(blk, prev(A_HEADS)),
                  pl.BlockSpec(blk, cur(2 * A_HEADS)), pl.BlockSpec(blk, prev(2 * A_HEADS))],
        out_specs=pl.BlockSpec(blk, lambda b, h, i: (b, i, h)),
        out_shape=jax.ShapeDtypeStruct((n, t, A_HEADS * HEAD_DIM), F32),
        scratch_shapes=[pltpu.VMEM((3, ATTN_SB, HEAD_DIM), F32), pltpu.VMEM((3, ATTN_SB, HEAD_DIM), F32)],
        compiler_params=_cparams("parallel", "parallel", "arbitrary"), name="attn_prompt",
    )(proj, proj, proj, proj, proj)


def _attn_sample_kernel(x_ref, k1_ref, k2_ref, v1_ref, v2_ref, o_ref, *, scale):
    tn = 4
    nq = tn * A_HEADS
    nk = A_KEYS * A_HEADS

    def grid_masks(cols):
        qrow = lax.broadcasted_iota(jnp.int32, (nq, cols), 0)
        kcol = lax.broadcasted_iota(jnp.int32, (nq, cols), 1)
        return (qrow % A_HEADS) == (kcol % A_HEADS), qrow // A_HEADS, kcol // A_HEADS

    same_head, qt, key = grid_masks(nk)
    mult_dense = jnp.where(same_head & (key >= qt), 1.0, 0.0)
    mult_res = [jnp.where(same_head & (qt == t), 1.0, 0.0) for t in range(tn)]
    same_head, qt, kt = grid_masks(nq)
    mult_new = jnp.where(same_head & (kt == qt), float(len(A_DILATIONS)), jnp.where(same_head & (kt < qt), 1.0, 0.0))
    flat = lambda a: a.reshape(nk, HEAD_DIM)
    dense = slice(A_KEYS - A_KEYS // 4, A_KEYS)

    for b in range(x_ref.shape[0]):
        x = x_ref[b]
        q = (x[:, 0:A_HEADS] * scale).reshape(nq, HEAD_DIM)
        knew = x[:, A_HEADS:2 * A_HEADS].reshape(nq, HEAD_DIM)
        vnew = x[:, 2 * A_HEADS:3 * A_HEADS].reshape(nq, HEAD_DIM)
        segs = [(flat(k1_ref[b, dense]), flat(v1_ref[b, dense]), mult_dense)]
        for kr, vr in ((k1_ref, v1_ref), (k2_ref, v2_ref)):
            for t in range(tn):
                segs.append((flat(kr[b, :, t]), flat(vr[b, :, t]), mult_res[t]))
        segs.append((knew, vnew, mult_new))

        scores = [_dot_nt(q, k) for k, _, _ in segs]
        m = None
        for s, (_, _, mult) in zip(scores, segs):
            ms = jnp.max(jnp.where(mult > 0.0, s, NEG), axis=-1, keepdims=True)
            m = ms if m is None else jnp.maximum(m, ms)
        probs = [jnp.where(mult > 0.0, jnp.exp(s - m), 0.0) * mult for s, (_, _, mult) in zip(scores, segs)]
        l = sum(jnp.sum(p, axis=-1, keepdims=True) for p in probs)
        acc = sum(_dot(p, v) for p, (_, v, _) in zip(probs, segs))
        o_ref[b] = (acc / l).reshape(tn, A_HEADS, HEAD_DIM)


def _attn_sample(x4, cache_k, cache_v):
    n, tn = x4.shape[0], x4.shape[1]
    w = cache_k.shape[1]
    assert w == A_WMAX and tn == 4
    views = []
    for cache in (cache_k, cache_v):
        views += [cache.reshape(n, w // 4, 4, A_HEADS, HEAD_DIM), cache.reshape(n, w // 16, 16, A_HEADS, HEAD_DIM)]
    k1, k2, v1, v2 = views
    nb = SAMPLE_ATTN_SEQS
    s1 = pl.BlockSpec((nb, A_KEYS, 4, A_HEADS, HEAD_DIM), lambda b: (b, w // 4 // A_KEYS - 1, 0, 0, 0))
    s2 = pl.BlockSpec((nb, A_KEYS, 4, A_HEADS, HEAD_DIM), lambda b: (b, 0, 0, 0, 0))
    return pl.pallas_call(
        functools.partial(_attn_sample_kernel, scale=HEAD_DIM ** -0.5),
        grid=(n // nb,),
        in_specs=[pl.BlockSpec((nb, tn, 3 * A_HEADS, HEAD_DIM), lambda b: (b, 0, 0, 0)), s1, s2, s1, s2],
        out_specs=pl.BlockSpec((nb, tn, A_HEADS, HEAD_DIM), lambda b: (b, 0, 0, 0)),
        out_shape=jax.ShapeDtypeStruct((n, tn, A_HEADS, HEAD_DIM), F32),
        compiler_params=_cparams("parallel"), name="attn_sample",
    )(x4, k1, k2, v1, v2)


def _unit_lower_inverse_minus_identity(low):
    c = low.shape[-1]
    base = min(16, c)
    row = lax.broadcasted_iota(jnp.int32, (1, c, c), 1)
    col = lax.broadcasted_iota(jnp.int32, (1, c, c), 2)
    nil = jnp.where((row // base) == (col // base), -low, 0.0)
    q = nil
    pw = nil
    for _ in range(int(math.log2(base)) - 1):
        pw = _bmm(pw, pw)
        q = q + pw + _bmm(q, pw)
    b = base
    while b < c:
        sib = ((row // (2 * b)) == (col // (2 * b))) & ((row // b) != (col // b))
        off = jnp.where(sib, low, 0.0)
        t = off + _bmm(q, off)
        q = q - t - _bmm(t, q)
        b *= 2
    return q


def _dn_prompt_kernel(x_ref, ba_ref, cw_ref, adt_ref, cinit_ref, sinit_ref, o_ref, sout_ref,
                      xbuf, s_ref, *, c, nb):
    ci = pl.program_id(0)
    halo = 8
    hk = B_HEADS * B_DK
    pairs = [(b, h) for b in range(nb) for h in range(B_HEADS)]

    @pl.when(ci == 0)
    def _():
        xbuf[:, 0:halo, :] = cinit_ref[...]
        s_ref[...] = sinit_ref[...].reshape(s_ref.shape)

    xbuf[:, halo:halo + c, :] = x_ref[...]
    cw = cw_ref[...]
    y = cw[0:1, :] * xbuf[:, pl.ds(halo - B_CONV + 1, c), :]
    for i in range(1, B_CONV):
        y = y + cw[i:i + 1, :] * xbuf[:, pl.ds(halo - B_CONV + 1 + i, c), :]
    xbuf[:, 0:halo, :] = xbuf[:, c:c + halo, :]
    cq = y * _sigmoid(y)

    ba = ba_ref[...]
    beta_all = _sigmoid(ba)
    g_all = -jnp.exp(adt_ref[0:1, :]) * _softplus(ba + adt_ref[1:2, :])
    row = lax.broadcasted_iota(jnp.int32, (c, c), 0)
    col = lax.broadcasted_iota(jnp.int32, (c, c), 1)
    tri_l = (row >= col).astype(F32)
    tri_u = (row <= col).astype(F32)
    gcum_col = [_dot_f32(tri_l, g_all[b]) for b in range(nb)]
    gcum_row = [_dot_tn_f32(g_all[b], tri_u) for b in range(nb)]

    def heads(off):
        return jnp.stack([cq[b, :, off + h * B_DK:off + (h + 1) * B_DK] for b, h in pairs])

    q, k, v = heads(0), heads(hk), heads(2 * hk)
    q = q * lax.rsqrt(jnp.sum(q * q, axis=-1, keepdims=True) + EPS) * (B_DK ** -0.5)
    k = k * lax.rsqrt(jnp.sum(k * k, axis=-1, keepdims=True) + EPS)
    bl, al = BA_LANE, BA_LANE + B_HEADS
    beta = jnp.stack([beta_all[b, :, bl + h:bl + h + 1] for b, h in pairs])
    gc = jnp.stack([gcum_col[b][:, al + h:al + h + 1] for b, h in pairs])
    gr = jnp.stack([gcum_row[b][al + h:al + h + 1, :] for b, h in pairs])
    decay = jnp.exp(jnp.where((row >= col)[None], gc - gr, NEG))
    kb = k * beta
    low = jnp.where((row > col)[None], _bmm_nt(kb, k) * decay, 0.0)
    qinv = _unit_lower_inverse_minus_identity(low)
    eg = jnp.exp(gc)
    rhs = jnp.concatenate([v * beta, kb * eg], axis=2)
    sol = rhs + _bmm(qinv, rhs)
    u = sol[:, :, 0:B_DK]
    w = sol[:, :, B_DK:2 * B_DK]
    attn = _bmm_nt(q, k) * decay
    glast = gc[:, c - 1:c, :]
    kd = k * jnp.exp(glast - gc)
    s = s_ref[...]
    v_new = u - _bmm(w, s)
    o = _bmm(q * eg, s) + _bmm(attn, v_new)
    s_ref[...] = s * jnp.exp(glast) + _bmm_tn(kd, v_new)
    for b in range(nb):
        o_ref[b] = jnp.concatenate([o[b * B_HEADS + h] for h in range(B_HEADS)], axis=1)

    @pl.when(ci == pl.num_programs(0) - 1)
    def _():
        sout_ref[...] = s_ref[...].reshape(sout_ref.shape)


def _dn_prompt(proj, conv_w, adt, conv_init, s_init):
    n, t, _ = proj.shape
    c = math.gcd(t, DN_CHUNK)
    width = 3 * B_HEADS * B_DK
    return pl.pallas_call(
        functools.partial(_dn_prompt_kernel, c=c, nb=n),
        grid=(t // c,),
        in_specs=[pl.BlockSpec((n, c, width), lambda i: (0, i, 1)),
                  pl.BlockSpec((n, c, LANES), lambda i: (0, i, proj.shape[2] // LANES - 1)),
                  pl.BlockSpec((B_CONV, width), lambda i: (0, 0)),
                  pl.BlockSpec((2, LANES), lambda i: (0, 0)),
                  pl.BlockSpec((n, 8, width), lambda i: (0, 0, 0)),
                  pl.BlockSpec((n, B_HEADS, B_DK, B_DK), lambda i: (0, 0, 0, 0))],
        out_specs=[pl.BlockSpec((n, c, B_HEADS * B_DK), lambda i: (0, i, 0)),
                   pl.BlockSpec((n, B_HEADS, B_DK, B_DK), lambda i: (0, 0, 0, 0))],
        out_shape=[jax.ShapeDtypeStruct((n, t, B_HEADS * B_DK), F32),
                   jax.ShapeDtypeStruct((n, B_HEADS, B_DK, B_DK), F32)],
        scratch_shapes=[pltpu.VMEM((n, 8 + c, width), F32), pltpu.VMEM((n * B_HEADS, B_DK, B_DK), F32)],
        compiler_params=_cparams("arbitrary"), name="deltanet_prompt",
    )(proj, proj, conv_w, adt, conv_init, s_init)


DN_SAMPLE_SEQS = 8


def _dn_sample_kernel(x_ref, bat_ref, cw_ref, adt_ref, cs_ref, sinit_ref, o_ref, s_ref):
    tn = 4
    hk = B_HEADS * B_DK
    seqs = range(DN_SAMPLE_SEQS)
    xp = jnp.concatenate([cs_ref[...], x_ref[...]], axis=1)
    cw = cw_ref[...]
    y = cw[0] * xp[:, 0:tn]
    for i in range(1, B_CONV):
        y = y + cw[i] * xp[:, i:i + tn]
    cq = y * _sigmoid(y)
    q = cq[:, :, 0:B_HEADS]
    k = cq[:, :, B_HEADS:2 * B_HEADS]
    v = cq[:, :, 2 * B_HEADS:3 * B_HEADS]
    q = q * lax.rsqrt(jnp.sum(q * q, axis=-1, keepdims=True) + EPS) * (B_DK ** -0.5)
    k = k * lax.rsqrt(jnp.sum(k * k, axis=-1, keepdims=True) + EPS)
    bat = bat_ref[...]
    beta = _sigmoid(bat[:, BA_LANE:BA_LANE + B_HEADS, :])
    a = jnp.exp(-jnp.exp(adt_ref[:, 0:1]) * _softplus(bat[:, BA_LANE + B_HEADS:LANES, :] + adt_ref[:, 1:2]))
    lane_head = lax.broadcasted_iota(jnp.int32, (B_HEADS, hk), 1) // B_DK
    head_mask = lane_head == lax.broadcasted_iota(jnp.int32, (B_HEADS, hk), 0)

    def block_diag(x):
        return jnp.where(head_mask, jnp.concatenate([x] * B_HEADS, axis=1), 0.0)

    s_ref[...] = sinit_ref[...]
    for t in range(tn):
        kbd = [block_diag(k[b, t]) for b in seqs]
        ks = [_dot(kbd[b], s_ref[b]) for b in seqs]
        w = [beta[b, :, t:t + 1] * (v[b, t] - a[b, :, t:t + 1] * ks[b]) for b in seqs]
        upd = [_dot_tn(kbd[b], w[b]) for b in seqs]
        for b in seqs:
            for h in range(B_HEADS):
                rows = slice(h * B_DK, (h + 1) * B_DK)
                s_ref[b, rows, :] = s_ref[b, rows, :] * a[b, h:h + 1, t:t + 1] + upd[b][rows, :]
        for b in seqs:
            o_ref[b, t] = _dot(block_diag(q[b, t]), s_ref[b])


def _dn_sample(x4, bat, conv_w, adt_col, conv_state, s_init):
    n = x4.shape[0]
    nb = DN_SAMPLE_SEQS
    rows = 3 * B_HEADS
    hk = B_HEADS * B_DK
    return pl.pallas_call(
        _dn_sample_kernel,
        grid=(n // nb,),
        in_specs=[pl.BlockSpec((nb, 4, rows, B_DK), lambda b: (b, 0, 1, 0)),
                  pl.BlockSpec((nb, LANES, 4), lambda b: (b, 0, 0)),
                  pl.BlockSpec((B_CONV, rows, B_DK), lambda b: (0, 0, 0)),
                  pl.BlockSpec((B_HEADS, 2), lambda b: (0, 0)),
                  pl.BlockSpec((nb, B_CONV - 1, rows, B_DK), lambda b: (b, 0, 0, 0)),
                  pl.BlockSpec((nb, hk, B_DK), lambda b: (b, 0, 0))],
        out_specs=[pl.BlockSpec((nb, 4, B_HEADS, B_DK), lambda b: (b, 0, 0, 0)),
                   pl.BlockSpec((nb, hk, B_DK), lambda b: (b, 0, 0))],
        out_shape=[jax.ShapeDtypeStruct((n, 4, B_HEADS, B_DK), F32),
                   jax.ShapeDtypeStruct((n, hk, B_DK), F32)],
        compiler_params=_cparams("parallel"), name="deltanet_sample",
    )(x4, bat, conv_w, adt_col, conv_state, s_init)


def _gla_kernel(q_ref, k_ref, v_ref, glr_ref, w2_ref, gb_ref, sinit_ref, o_ref, sout_ref,
                s_ref, *, c_real, c, dk, dv, nb):
    ci = pl.program_id(1)
    sub = min(GLA_SUB, c)

    @pl.when(ci == 0)
    def _():
        s_ref[...] = sinit_ref[...]

    def load(ref, b):
        x = ref[b]
        if c_real < c:
            x = jnp.concatenate([x, jnp.zeros((c - c_real, x.shape[1]), F32)], axis=0)
        return x

    row = lax.broadcasted_iota(jnp.int32, (c, c), 0)
    col = lax.broadcasted_iota(jnp.int32, (c, c), 1)
    tri = (row >= col).astype(F32)
    row_valid = lax.broadcasted_iota(jnp.int32, (c, 1), 0) < c_real
    sub_row = lax.broadcasted_iota(jnp.int32, (sub, 1), 0)
    sub_col = lax.broadcasted_iota(jnp.int32, (sub, c), 1)
    key_row = lax.broadcasted_iota(jnp.int32, (c, 1), 0)
    ones = jnp.ones((c, LANES), F32)
    starts = range(0, c, sub)
    seqs = []
    for b in range(nb):
        pre = _dot(load(glr_ref, b), w2_ref[...]) + gb_ref[...]
        la = jnp.where(row_valid, (jnp.minimum(pre, 0.0) - jnp.log(1.0 + jnp.exp(-jnp.abs(pre)))) / C_GATE_TAU, 0.0)
        gc = _dot_f32(tri, la)
        seqs.append((la, gc, load(q_ref, b) * (dk ** -0.5), load(k_ref, b), load(v_ref, b)))
    spans = [gc[lo:lo + 1, :] - gc[lo + sub - 1:lo + sub, :] for _, gc, _, _, _ in seqs for lo in starts]
    safe = jnp.max(functools.reduce(jnp.maximum, spans)) <= GLA_SAFE_SPAN

    def head(b, h):
        la, gc, q, k, v = seqs[b]
        ks = slice(h * dk, (h + 1) * dk)
        return la[:, ks], gc[:, ks], q[:, ks], k[:, ks], v[:, h * dv:(h + 1) * dv]

    for b in range(nb):
        outs = []
        for h in range(C_HEADS):
            la, gc, q, k, v = head(b, h)
            rows = []
            for lo in starts:
                ref_pt = gc[lo:lo + 1, :]
                qsc = q[lo:lo + sub, :] * jnp.exp(gc[lo:lo + sub, :] - ref_pt)
                limit = jnp.where(safe, lo + sub, lo)
                ksc = jnp.where(key_row < limit, k * jnp.exp(jnp.minimum(ref_pt - gc, GLA_SAFE_SPAN)), 0.0)
                rows.append(_dot_nt(qsc, ksc))
            attn = jnp.where(row >= col, jnp.concatenate(rows, axis=0), 0.0)
            s = s_ref[b, h]
            o = _dot(attn, v) + _dot(q * jnp.exp(gc), s)
            glast = gc[c - 1:c, :]
            kd = k * jnp.exp(glast - gc)
            gl_col = jnp.exp(_dot_tn_f32(la, ones))[:, 0:1]
            s_ref[b, h] = s * gl_col + _dot_tn(kd, v)
            outs.append(o[0:c_real, :])
        o_ref[b] = jnp.concatenate(outs, axis=1)

    @pl.when(jnp.logical_not(safe))
    def _():
        for b in range(nb):
            fixes = []
            for h in range(C_HEADS):
                _, gc, q, k, v = head(b, h)
                rows = []
                for lo in starts:
                    qb = q[lo:lo + sub, :]
                    gblk = gc[lo:lo + sub, :]
                    blk = jnp.zeros((sub, c), F32)
                    for jj in range(sub):
                        j = lo + jj
                        e = jnp.exp(jnp.minimum(gblk - gc[j:j + 1, :], 0.0))
                        a = jnp.sum(qb * k[j:j + 1, :] * e, axis=-1, keepdims=True)
                        blk = jnp.where((sub_col == j) & (sub_row >= jj), a, blk)
                    rows.append(blk)
                fixes.append(_dot(jnp.concatenate(rows, axis=0), v)[0:c_real, :])
            o_ref[b] += jnp.concatenate(fixes, axis=1)

    @pl.when(ci == pl.num_programs(1) - 1)
    def _():
        sout_ref[...] = s_ref[...]


def _gla(proj, w2, gb, s_init, nb):
    n, t, _ = proj.shape
    dk, dv = s_init.shape[2], s_init.shape[3]
    c_real = math.gcd(t, GLA_CHUNK)
    c = max(c_real, 8)
    hk, hv = C_HEADS * dk, C_HEADS * dv
    return pl.pallas_call(
        functools.partial(_gla_kernel, c_real=c_real, c=c, dk=dk, dv=dv, nb=nb),
        grid=(n // nb, t // c_real),
        in_specs=[pl.BlockSpec((nb, c_real, hk), lambda b, i: (b, i, 0)),
                  pl.BlockSpec((nb, c_real, hk), lambda b, i: (b, i, 1)),
                  pl.BlockSpec((nb, c_real, hv), lambda b, i: (b, i, 2 * hk // hv)),
                  pl.BlockSpec((nb, c_real, LANES), lambda b, i: (b, i, (2 * hk + 2 * hv) // LANES)),
                  pl.BlockSpec((LANES, hk), lambda b, i: (0, 0)),
                  pl.BlockSpec((1, hk), lambda b, i: (0, 0)),
                  pl.BlockSpec((nb, C_HEADS, dk, dv), lambda b, i: (b, 0, 0, 0))],
        out_specs=[pl.BlockSpec((nb, c_real, hv), lambda b, i: (b, i, 0)),
                   pl.BlockSpec((nb, C_HEADS, dk, dv), lambda b, i: (b, 0, 0, 0))],
        out_shape=[jax.ShapeDtypeStruct((n, t, hv), F32),
                   jax.ShapeDtypeStruct((n, C_HEADS, dk, dv), F32)],
        scratch_shapes=[pltpu.VMEM((nb, C_HEADS, dk, dv), F32)],
        compiler_params=_cparams("parallel", "arbitrary"), name="gla",
    )(proj, proj, proj, proj, w2, gb, s_init)


def _gated_headnorm(o, z, nw, heads, width):
    parts = []
    for h in range(heads):
        zz = z[:, h * width:(h + 1) * width]
        parts.append(_rms(o[:, h * width:(h + 1) * width], nw) * (zz * _sigmoid(zz)))
    return jnp.concatenate(parts, axis=1)


def _row_parts(tm):
    tp = tm // MIX_PARTS if tm % (8 * MIX_PARTS) == 0 else tm
    return [slice(lo, lo + tp) for lo in range(0, tm, tp)]


def _mix_even_kernel(oa_ref, ob_ref, z_ref, nw_ref, w_ref, r_ref, g_ref, out_ref):
    ka = A_HEADS * HEAD_DIM
    for rows in _row_parts(out_ref.shape[0]):
        obn = _gated_headnorm(ob_ref[rows, :], z_ref[rows, :], nw_ref[...], B_HEADS, B_DK)
        m = _dot(oa_ref[rows, :], w_ref[0:ka, :]) + _dot(obn, w_ref[ka:, :])
        out_ref[rows, :] = r_ref[rows, :] + _rms(m, g_ref[...])


def _mix_even(o_a, o_b, proj, nw, w_out, r, g_post, tm):
    m, d = r.shape
    ka = o_a.shape[1]
    kb = o_b.shape[1]
    z_block = (2 * 3 * A_HEADS * HEAD_DIM) // kb
    return pl.pallas_call(
        _mix_even_kernel, grid=(m // tm,),
        in_specs=[pl.BlockSpec((tm, ka), lambda i: (i, 0)),
                  pl.BlockSpec((tm, kb), lambda i: (i, 0)),
                  pl.BlockSpec((tm, kb), lambda i: (i, z_block)),
                  pl.BlockSpec((1, B_DK), lambda i: (0, 0)),
                  pl.BlockSpec((ka + kb, d), lambda i: (0, 0)),
                  pl.BlockSpec((tm, d), lambda i: (i, 0)),
                  pl.BlockSpec((1, d), lambda i: (0, 0))],
        out_specs=pl.BlockSpec((tm, d), lambda i: (i, 0)),
        out_shape=jax.ShapeDtypeStruct((m, d), F32),
        compiler_params=_cparams("parallel"), name="mix_even",
    )(o_a, o_b, proj, nw.reshape(1, -1), w_out, r, g_post.reshape(1, d))


def _mix_odd_kernel(o_ref, z_ref, nw_ref, w_ref, r_ref, g_ref, out_ref, *, dv):
    for rows in _row_parts(out_ref.shape[0]):
        on = _gated_headnorm(o_ref[rows, :], z_ref[rows, :], nw_ref[...], C_HEADS, dv)
        out_ref[rows, :] = r_ref[rows, :] + _rms(_dot(on, w_ref[...]), g_ref[...])


def _mix_odd(o, proj, nw, w_out, r, g_post, tm):
    m, d = r.shape
    kv = o.shape[1]
    return pl.pallas_call(
        functools.partial(_mix_odd_kernel, dv=kv // C_HEADS), grid=(m // tm,),
        in_specs=[pl.BlockSpec((tm, kv), lambda i: (i, 0)),
                  pl.BlockSpec((tm, kv), lambda i: (i, 2)),
                  pl.BlockSpec((1, kv // C_HEADS), lambda i: (0, 0)),
                  pl.BlockSpec((kv, d), lambda i: (0, 0)),
                  pl.BlockSpec((tm, d), lambda i: (i, 0)),
                  pl.BlockSpec((1, d), lambda i: (0, 0))],
        out_specs=pl.BlockSpec((tm, d), lambda i: (i, 0)),
        out_shape=jax.ShapeDtypeStruct((m, d), F32),
        compiler_params=_cparams("parallel"), name="mix_odd",
    )(o, proj, nw.reshape(1, -1), w_out, r, g_post.reshape(1, d))


def _ffn_kernel(x_ref, gpre_ref, wg_ref, wv_ref, cwg_ref, cwv_ref, wd_ref, gpost_ref, ig_ref, iv_ref,
                o_ref, tg_ref, tv_ref, h_ref, ug, uv, *carry, tm, halo, shift):
    i = pl.program_id(1)
    j = pl.program_id(2)

    @pl.when(j == 0)
    def _():
        h_ref[...] = _rms(x_ref[0], gpre_ref[...]).astype(BF16)
        o_ref[0] = jnp.zeros_like(o_ref[0])

    if carry:
        cg, cv = carry

        @pl.when(i == 0)
        def _():
            ug[0:halo, :] = ig_ref[0]
            uv[0:halo, :] = iv_ref[0]

        @pl.when(i > 0)
        def _():
            ug[0:halo, :] = cg[j]
            uv[0:halo, :] = cv[j]
    else:
        ug[0:halo, :] = ig_ref[0]
        uv[0:halo, :] = iv_ref[0]

    tf = ug.shape[1]
    sub = min(FFN_SUB, tf)
    cols = [slice(s * sub, (s + 1) * sub) for s in range(tf // sub)]
    for cs in cols:
        ug[halo:halo + tm, cs] = jnp.dot(h_ref[...], wg_ref[:, cs], preferred_element_type=F32)
        uv[halo:halo + tm, cs] = jnp.dot(h_ref[...], wv_ref[:, cs], preferred_element_type=F32)

    def conv(u, cw_ref, cs):
        y = cw_ref[0:1, cs] * u[pl.ds(halo - (FFN_CONV - 1) * shift, tm), cs]
        for tap in range(1, FFN_CONV):
            y = y + cw_ref[tap:tap + 1, cs] * u[pl.ds(halo - (FFN_CONV - 1 - tap) * shift, tm), cs]
        return y

    acc = None
    for cs in cols:
        act = _gelu_tanh(conv(ug, cwg_ref, cs)) * conv(uv, cwv_ref, cs)
        part = _dot(act, wd_ref[cs, :])
        acc = part if acc is None else acc + part
    tail_g = ug[tm:tm + halo, :]
    tail_v = uv[tm:tm + halo, :]
    tg_ref[0, 0] = tail_g
    tv_ref[0, 0] = tail_v
    if carry:
        cg[j] = tail_g
        cv[j] = tail_v
    o_ref[0] += acc

    @pl.when(j == pl.num_programs(2) - 1)
    def _():
        o_ref[0] = x_ref[0] + _rms(o_ref[0], gpost_ref[...])


def _ffn(x, g_pre, w_up, conv_w, w_down, g_post, init, layer, tm, tf, shift):
    n, t, d = x.shape
    dff = w_down.shape[1]
    halo = init.shape[1]
    ni, nj = t // tm, dff // tf
    scratch = [pltpu.VMEM((tm, d), BF16), pltpu.VMEM((halo + tm, tf), F32), pltpu.VMEM((halo + tm, tf), F32)]
    if ni > 1:
        scratch += [pltpu.VMEM((nj, halo, tf), F32), pltpu.VMEM((nj, halo, tf), F32)]
    out, tail_g, tail_v = pl.pallas_call(
        functools.partial(_ffn_kernel, tm=tm, halo=halo, shift=shift),
        grid=(n, ni, nj),
        in_specs=[pl.BlockSpec((1, tm, d), lambda b, i, j: (b, i, 0)),
                  pl.BlockSpec((1, d), lambda b, i, j: (0, 0)),
                  pl.BlockSpec((None, d, tf), lambda b, i, j: (layer, 0, j)),
                  pl.BlockSpec((None, d, tf), lambda b, i, j: (layer, 0, j + nj)),
                  pl.BlockSpec((None, FFN_CONV, tf), lambda b, i, j: (layer, 0, j)),
                  pl.BlockSpec((None, FFN_CONV, tf), lambda b, i, j: (layer, 0, j + nj)),
                  pl.BlockSpec((None, tf, d), lambda b, i, j: (layer, j, 0)),
                  pl.BlockSpec((1, d), lambda b, i, j: (0, 0)),
                  pl.BlockSpec((1, halo, tf), lambda b, i, j: (b, 0, j)),
                  pl.BlockSpec((1, halo, tf), lambda b, i, j: (b, 0, j + nj))],
        out_specs=[pl.BlockSpec((1, tm, d), lambda b, i, j: (b, i, 0), pipeline_mode=pl.Buffered(1)),
                   pl.BlockSpec((1, 1, halo, tf), lambda b, i, j: (b, i, 0, j)),
                   pl.BlockSpec((1, 1, halo, tf), lambda b, i, j: (b, i, 0, j))],
        out_shape=[jax.ShapeDtypeStruct((n, t, d), F32),
                   jax.ShapeDtypeStruct((n, ni, halo, dff), F32),
                   jax.ShapeDtypeStruct((n, ni, halo, dff), F32)],
        scratch_shapes=scratch,
        compiler_params=_cparams("parallel", "arbitrary", "arbitrary"), name="conv_ffn",
    )(x, g_pre.reshape(1, d), w_up, w_up, conv_w, conv_w, w_down, g_post.reshape(1, d), init, init)
    return out, jnp.concatenate([tail_g[:, ni - 1], tail_v[:, ni - 1]], axis=-1)


def _ple_kernel(r_ref, p_ref, wp_ref, wg_ref, o_ref):
    r = r_ref[...]
    o_ref[...] = r + _dot(p_ref[...], wp_ref[...]) * _sigmoid(_dot(r, wg_ref[...]))


def _ple(r, p, w_proj, w_gate, layer, tm):
    m, d = r.shape
    pd = p.shape[1]
    return pl.pallas_call(
        _ple_kernel, grid=(m // tm,),
        in_specs=[pl.BlockSpec((tm, d), lambda i: (i, 0)),
                  pl.BlockSpec((tm, pd), lambda i: (i, 0)),
                  pl.BlockSpec((None, pd, d), lambda i: (layer, 0, 0)),
                  pl.BlockSpec((None, d, d), lambda i: (layer, 0, 0))],
        out_specs=pl.BlockSpec((tm, d), lambda i: (i, 0)),
        out_shape=jax.ShapeDtypeStruct((m, d), F32),
        compiler_params=_cparams("parallel"), name="ple",
    )(r, p, w_proj, w_gate)


def _reorder_cols(w, spans, tn):
    parts = [w[:, a:b].astype(BF16) for a, b in spans]
    width = sum(p.shape[1] for p in parts)
    parts.append(jnp.zeros((w.shape[0], -width % tn), BF16))
    return jnp.concatenate(parts, axis=1)


def _prep_weights(w_in_even, w_out_even, w_in_odd, w_out_odd, gla_gate_w2, ffn_w_up, ffn_w_down,
                  ple_w_proj, ple_w_gate, dn_a_log, dn_dt_bias):
    assert (3 * A_HEADS * HEAD_DIM + 3 * B_HEADS * B_DK) % PROJ0_TN == 0
    w0 = _reorder_cols(w_in_even[0], [(0, None)], PROJ0_TN)
    assert w0.shape[1] - w_in_even.shape[2] >= LANES - 2 * B_HEADS
    d = w_in_odd.shape[1]
    qkv = 2 * (d // 2) + d
    w1 = _reorder_cols(w_in_odd[0], [(0, qkv), (qkv + C_GATE_RANK, None), (qkv, qkv + C_GATE_RANK)], PROJ_TN)
    w2 = jnp.pad(gla_gate_w2[0], ((0, LANES - C_GATE_RANK), (0, 0))).astype(BF16)
    adt_row = jnp.pad(jnp.stack([dn_a_log[0], dn_dt_bias[0]]), ((0, 0), (BA_LANE + B_HEADS, 0)))
    adt_col = jnp.stack([dn_a_log[0], dn_dt_bias[0]], axis=1)
    return dict(w0=w0, w1=w1, w2=w2, adt_row=adt_row, adt_col=adt_col,
                w_out_even=w_out_even[0].astype(BF16), w_out_odd=w_out_odd[0].astype(BF16),
                w_up=ffn_w_up.astype(BF16), w_down=ffn_w_down.astype(BF16),
                w_proj=ple_w_proj.astype(BF16), w_gate=ple_w_gate.astype(BF16))


def _layer_tail(r, p, layer, wt, prm, ffn_init, n, t, tm):
    d = r.shape[1]
    r, tail = _ffn(r.reshape(n, t, d), prm["norm_ffn_pre"][layer], wt["w_up"], prm["ffn_conv_w"],
                   wt["w_down"], prm["norm_ffn_post"][layer], ffn_init, layer, PROMPT_FFN_TM, PROMPT_FFN_TF, 1)
    r = _ple(r.reshape(n * t, d), p, wt["w_proj"], wt["w_gate"], layer, tm)
    return r, tail


def _prompt_group(x, p, wt, prm):
    n, t, d = x.shape
    m = n * t
    tm = MIX_TM
    dff2 = prm["ffn_conv_w"].shape[-1]
    r = x.reshape(m, d)
    proj = _rms_matmul(r, prm["norm_mix_pre"][0], wt["w0"], PROMPT_PROJ_TM, PROJ0_TN, lead=2 * B_HEADS)
    pv = proj.reshape(n, t, -1)
    o_a = _attn_prompt(pv, n, t).reshape(m, -1)
    width = 3 * B_HEADS * B_DK
    o_b, dn_state = _dn_prompt(pv, prm["dn_conv_w"][0], wt["adt_row"], jnp.zeros((n, 8, width), F32),
                               jnp.zeros((n, B_HEADS, B_DK, B_DK), F32))
    r = _mix_even(o_a, o_b.reshape(m, -1), proj, prm["dn_norm_w"][0], wt["w_out_even"], r,
                  prm["norm_mix_post"][0], tm)
    keep = min(A_WMAX, t)
    hd = A_HEADS * HEAD_DIM
    win_k = pv[:, t - keep:, hd:2 * hd].reshape(n, keep, A_HEADS, HEAD_DIM)
    win_v = pv[:, t - keep:, 2 * hd:3 * hd].reshape(n, keep, A_HEADS, HEAD_DIM)
    dn_conv = pv[:, t - (B_CONV - 1):, width:2 * width]
    ffn_zero = jnp.zeros((n, 8, dff2), F32)
    r, tail0 = _layer_tail(r, p[0].reshape(m, -1), 0, wt, prm, ffn_zero, n, t, tm)
    proj1 = _rms_matmul(r, prm["norm_mix_pre"][1], wt["w1"], PROMPT_PROJ_TM, PROJ_TN)
    o_c, gla_state = _gla(proj1.reshape(n, t, -1), wt["w2"], prm["gla_gate_b"][0].reshape(1, -1),
                          jnp.zeros((n, C_HEADS, d // (2 * C_HEADS), d // C_HEADS), F32), n)
    r = _mix_odd(o_c.reshape(m, -1), proj1, prm["gla_norm_w"][0], wt["w_out_odd"], r, prm["norm_mix_post"][1], tm)
    r, tail1 = _layer_tail(r, p[1].reshape(m, -1), 1, wt, prm, ffn_zero, n, t, tm)
    ffn_conv = jnp.stack([tail0[:, 8 - (FFN_CONV - 1):], tail1[:, 8 - (FFN_CONV - 1):]])
    return (r.reshape(n, t, d), win_k[None], win_v[None], dn_conv[None], dn_state[None], gla_state[None], ffn_conv)


def _sample_group(x, p, cache_k, cache_v, dn_conv_state, dn_state, gla_state, ffn_state, wt, prm):
    n, t, d = x.shape
    m = n * t
    tm = MIX_TM
    r = x.reshape(m, d)

    def time_major(a):
        return a.reshape(n, t, -1).transpose(1, 0, 2).reshape(1, m, -1)

    def seq_major(a):
        return a.reshape(t, n, -1).transpose(1, 0, 2).reshape(m, -1)

    def ffn_layer(r, layer):
        init = ffn_state[layer].transpose(1, 0, 2).reshape(1, (FFN_CONV - 1) * n, -1)
        rt, tail = _ffn(time_major(r), prm["norm_ffn_pre"][layer], wt["w_up"], prm["ffn_conv_w"],
                        wt["w_down"], prm["norm_ffn_post"][layer], init, layer, m, SAMPLE_FFN_TF, n)
        r = _ple(seq_major(rt), p[layer].reshape(m, -1), wt["w_proj"], wt["w_gate"], layer, tm)
        return r, tail.reshape(FFN_CONV - 1, n, -1).transpose(1, 0, 2)

    proj = _rms_matmul(r, prm["norm_mix_pre"][0], wt["w0"], m, PROJ0_TN, lead=2 * B_HEADS)
    pv = proj.reshape(n, t, -1)
    width = 3 * B_HEADS * B_DK
    heads = pv.reshape(n, t, -1, HEAD_DIM)
    o_a = _attn_sample(heads, cache_k[0], cache_v[0])
    bat = heads[:, :, heads.shape[2] - 1].transpose(0, 2, 1)
    o_b, dn_new = _dn_sample(heads, bat, prm["dn_conv_w"][0].reshape(B_CONV, -1, B_DK), wt["adt_col"],
                             dn_conv_state[0].reshape(n, B_CONV - 1, -1, B_DK),
                             dn_state[0].reshape(n, B_HEADS * B_DK, B_DK))
    r = _mix_even(o_a.reshape(m, -1), o_b.reshape(m, -1), proj, prm["dn_norm_w"][0], wt["w_out_even"], r,
                  prm["norm_mix_post"][0], tm)
    win_k = heads[:, :, A_HEADS:2 * A_HEADS]
    win_v = heads[:, :, 2 * A_HEADS:3 * A_HEADS]
    dn_conv = pv[:, t - (B_CONV - 1):, width:2 * width]
    r, tail0 = ffn_layer(r, 0)
    proj1 = _rms_matmul(r, prm["norm_mix_pre"][1], wt["w1"], m, PROJ_TN)
    o_c, gla_new = _gla(proj1.reshape(n, t, -1), wt["w2"], prm["gla_gate_b"][0].reshape(1, -1), gla_state[0],
                        SAMPLE_GLA_SEQS)
    r = _mix_odd(o_c.reshape(m, -1), proj1, prm["gla_norm_w"][0], wt["w_out_odd"], r, prm["norm_mix_post"][1], tm)
    r, tail1 = ffn_layer(r, 1)
    return (r.reshape(n, t, d), win_k[None], win_v[None], dn_conv[None],
            dn_new.reshape(dn_state.shape), gla_new[None], jnp.stack([tail0, tail1]))


def kernel(x_prompt, x_sample, cache_win_k, cache_win_v, state_dn_conv, state_dn, state_gla, state_ffn_conv, p_prompt, p_sample, norm_mix_pre, norm_mix_post, norm_ffn_pre, norm_ffn_post, w_in_even, w_out_even, dn_conv_w, dn_a_log, dn_dt_bias, dn_norm_w, w_in_odd, gla_gate_w2, gla_gate_b, gla_norm_w, w_out_odd, ffn_w_up, ffn_conv_w, ffn_w_down, ple_w_proj, ple_w_gate):
    wt = _prep_weights(w_in_even, w_out_even, w_in_odd, w_out_odd, gla_gate_w2, ffn_w_up, ffn_w_down,
                       ple_w_proj, ple_w_gate, dn_a_log, dn_dt_bias)
    prm = dict(norm_mix_pre=norm_mix_pre, norm_mix_post=norm_mix_post, norm_ffn_pre=norm_ffn_pre,
               norm_ffn_post=norm_ffn_post, dn_conv_w=dn_conv_w, dn_norm_w=dn_norm_w, gla_gate_b=gla_gate_b,
               gla_norm_w=gla_norm_w, ffn_conv_w=ffn_conv_w)
    yp, kp, vp, dcp, dsp, gsp, fcp = _prompt_group(x_prompt, p_prompt, wt, prm)
    ys, ks, vs, dcs, dss, gss, fcs = _sample_group(x_sample, p_sample, cache_win_k, cache_win_v, state_dn_conv,
                                                   state_dn, state_gla, state_ffn_conv, wt, prm)
    return (yp, ys, kp, vp, dcp, dsp, gsp, fcp, ks, vs, dcs, dss, gss, fcs)
```

```python
import functools
import math

import jax
import jax.numpy as jnp
from jax import lax
from jax.experimental import pallas as pl
from jax.experimental.pallas import tpu as pltpu

F32 = jnp.float32
BF16 = jnp.bfloat16
EPS = 1e-6
NEG = -1e30
HIGHEST = lax.Precision.HIGHEST

LANES = 128
V7X_VMEM_LIMIT_BYTES = 60 * 2**20

HEAD_DIM = 128
A_HEADS = 8
A_KEYS = 128
A_DILATIONS = (1, 4, 16)
A_WMAX = 2048
B_HEADS = 8
B_DK = 128
B_CONV = 4
C_HEADS = 4
C_GATE_RANK = 16
C_GATE_TAU = 16.0
FFN_CONV = 3
DN_CHUNK = 64
GLA_CHUNK = 64
GLA_SUB = 16
GLA_SAFE_SPAN = 60.0
PROJ_TN = 1280
PROJ0_TN = 1536
BA_LANE = 128 - 2 * 8
PROMPT_PROJ_TM = 1024
PROMPT_FFN_TM = 1024
PROMPT_FFN_TF = 512
FFN_SUB = 512
SAMPLE_FFN_TF = 512
WEIGHT_PREP_ROWS = 256
MIX_TM = 512
MIX_PARTS = 2
SAMPLE_GLA_SEQS = 4
SAMPLE_ATTN_SEQS = 2


def _cparams(*sem):
    return pltpu.CompilerParams(dimension_semantics=sem, vmem_limit_bytes=V7X_VMEM_LIMIT_BYTES)


def _dot(a, b):
    return jnp.dot(a.astype(BF16), b.astype(BF16), preferred_element_type=F32)


def _dot_nt(a, b):
    return lax.dot_general(a.astype(BF16), b.astype(BF16), (((1,), (1,)), ((), ())),
                           preferred_element_type=F32)


def _dot_tn(a, b):
    return lax.dot_general(a.astype(BF16), b.astype(BF16), (((0,), (0,)), ((), ())),
                           preferred_element_type=F32)


def _bmm(a, b):
    return lax.dot_general(a.astype(BF16), b.astype(BF16), (((2,), (1,)), ((0,), (0,))),
                           preferred_element_type=F32)


def _bmm_nt(a, b):
    return lax.dot_general(a.astype(BF16), b.astype(BF16), (((2,), (2,)), ((0,), (0,))),
                           preferred_element_type=F32)


def _bmm_tn(a, b):
    return lax.dot_general(a.astype(BF16), b.astype(BF16), (((1,), (1,)), ((0,), (0,))),
                           preferred_element_type=F32)


def _dot_f32(a, b):
    return jnp.dot(a, b, precision=HIGHEST, preferred_element_type=F32)


def _dot_tn_f32(a, b):
    return lax.dot_general(a, b, (((0,), (0,)), ((), ())), precision=HIGHEST,
                           preferred_element_type=F32)


def _rms(x, g):
    return x * lax.rsqrt(jnp.mean(x * x, axis=-1, keepdims=True) + EPS) * g


def _sigmoid(x):
    return 1.0 / (1.0 + jnp.exp(-x))


def _softplus(x):
    return jnp.maximum(x, 0.0) + jnp.log(1.0 + jnp.exp(-jnp.abs(x)))


def _gelu_tanh(x):
    return 0.5 * x * (1.0 + jnp.tanh(math.sqrt(2.0 / math.pi) * (x + 0.044715 * (x * x * x))))


def _rms_matmul_kernel(x_ref, g_ref, w_ref, o_ref, h_ref, *, lead):
    j = pl.program_id(1)

    @pl.when(j == 0)
    def _():
        h_ref[...] = _rms(x_ref[...], g_ref[...]).astype(BF16)

    acc = jnp.dot(h_ref[...], w_ref[...], preferred_element_type=F32)
    if lead:
        last = pl.num_programs(1) - 1

        @pl.when(j < last)
        def _():
            o_ref[...] = acc

        @pl.when(j == last)
        def _():
            o_ref[...] = pltpu.roll(acc, acc.shape[1] - lead, axis=1)
    else:
        o_ref[...] = acc


def _rms_matmul(x, g, w, tm, tn, lead=0):
    m, k = x.shape
    n = w.shape[1]
    return pl.pallas_call(
        functools.partial(_rms_matmul_kernel, lead=lead),
        grid=(m // tm, n // tn),
        in_specs=[pl.BlockSpec((tm, k), lambda i, j: (i, 0)),
                  pl.BlockSpec((1, k), lambda i, j: (0, 0)),
                  pl.BlockSpec((k, tn), lambda i, j: (0, j))],
        out_specs=pl.BlockSpec((tm, tn), lambda i, j: (i, j)),
        out_shape=jax.ShapeDtypeStruct((m, n), F32),
        scratch_shapes=[pltpu.VMEM((tm, k), BF16)],
        compiler_params=_cparams("parallel", "arbitrary"),
        name="rms_matmul",
    )(x, g.reshape(1, k), w)


ATTN_SB = A_WMAX


def _attn_prompt_kernel(q_ref, kc_ref, kp_ref, vc_ref, vp_ref, o_ref, o_s, l_s, *, scale):
    has_prev = pl.program_id(2) > 0
    row = lax.broadcasted_iota(jnp.int32, (A_KEYS, 2 * A_KEYS), 0)
    col = lax.broadcasted_iota(jnp.int32, (A_KEYS, 2 * A_KEYS), 1)
    cur_ok = (col >= A_KEYS) & (col - A_KEYS <= row)
    prev_ok = (col < A_KEYS) & (col >= row)

    def tile(ref, start, d):
        if d == 1:
            return ref[0, pl.ds(start, A_KEYS), :]
        return ref[0, pl.ds(start, A_KEYS, stride=d), :]

    for g, d in enumerate(A_DILATIONS):
        span = A_KEYS * d
        for r in range(d):
            for j in range(ATTN_SB // span):
                start = r + span * j
                q = tile(q_ref, start, d) * scale
                if j > 0:
                    kp, vp = tile(kc_ref, start - span, d), tile(vc_ref, start - span, d)
                    ok = cur_ok | prev_ok
                else:
                    kp, vp = tile(kp_ref, ATTN_SB - span + r, d), tile(vp_ref, ATTN_SB - span + r, d)
                    ok = cur_ok | (prev_ok & has_prev)
                s = _dot_nt(q, jnp.concatenate([kp, tile(kc_ref, start, d)], axis=0))
                s = jnp.where(ok, s, NEG)
                m = jnp.max(s, axis=-1, keepdims=True)
                p = jnp.exp(s - m)
                l = jnp.sum(p, axis=-1, keepdims=True)
                o = _dot(p, jnp.concatenate([vp, tile(vc_ref, start, d)], axis=0)) / l
                lse = jnp.broadcast_to(m + jnp.log(l), (A_KEYS, HEAD_DIM))
                if d == 1:
                    o_s[g, pl.ds(start, A_KEYS), :] = o
                    l_s[g, pl.ds(start, A_KEYS), :] = lse
                else:
                    o_s[g, pl.ds(start, A_KEYS, stride=d), :] = o
                    l_s[g, pl.ds(start, A_KEYS, stride=d), :] = lse
    la, lb, lc = l_s[0], l_s[1], l_s[2]
    m = jnp.maximum(jnp.maximum(la, lb), lc)
    wa, wb, wc = jnp.exp(la - m), jnp.exp(lb - m), jnp.exp(lc - m)
    o_ref[0] = (wa * o_s[0] + wb * o_s[1] + wc * o_s[2]) / (wa + wb + wc)


def _attn_prompt(proj, n, t):
    assert t % ATTN_SB == 0 and len(A_DILATIONS) == 3
    cur = lambda off: (lambda b, h, i: (b, i, off + h))
    prev = lambda off: (lambda b, h, i: (b, jnp.maximum(i - 1, 0), off + h))
    blk = (1, ATTN_SB, HEAD_DIM)
    return pl.pallas_call(
        functools.partial(_attn_prompt_kernel, scale=HEAD_DIM ** -0.5),
        grid=(n, A_HEADS, t // ATTN_SB),
        in_specs=[pl.BlockSpec(blk, cur(0)),
                  pl.BlockSpec(blk, cur(A_HEADS)), pl.BlockSpec(blk, prev(A_HEADS)),
                  pl.BlockSpec(blk, cur(2 * A_HEADS)), pl.BlockSpec(blk, prev(2 * A_HEADS))],
        out_specs=pl.BlockSpec(blk, lambda b, h, i: (b, i, h)),
        out_shape=jax.ShapeDtypeStruct((n, t, A_HEADS * HEAD_DIM), F32),
        scratch_shapes=[pltpu.VMEM((3, ATTN_SB, HEAD_DIM), F32), pltpu.VMEM((3, ATTN_SB, HEAD_DIM), F32)],
        compiler_params=_cparams("parallel", "parallel", "arbitrary"), name="attn_prompt",
    )(proj, proj, proj, proj, proj)


def _attn_sample_kernel(x_ref, k1_ref, k2_ref, v1_ref, v2_ref, o_ref, *, scale):
    tn = 4
    nq = tn * A_HEADS
    nk = A_KEYS * A_HEADS

    def grid_masks(cols):
        qrow = lax.broadcasted_iota(jnp.int32, (nq, cols), 0)
        kcol = lax.broadcasted_iota(jnp.int32, (nq, cols), 1)
        return (qrow % A_HEADS) == (kcol % A_HEADS), qrow // A_HEADS, kcol // A_HEADS

    same_head, qt, key = grid_masks(nk)
    mult_dense = jnp.where(same_head & (key >= qt), 1.0, 0.0)
    mult_res = [jnp.where(same_head & (qt == t), 1.0, 0.0) for t in range(tn)]
    same_head, qt, kt = grid_masks(nq)
    mult_new = jnp.where(same_head & (kt == qt), float(len(A_DILATIONS)), jnp.where(same_head & (kt < qt), 1.0, 0.0))
    flat = lambda a: a.reshape(nk, HEAD_DIM)
    dense = slice(A_KEYS - A_KEYS // 4, A_KEYS)

    for b in range(x_ref.shape[0]):
        x = x_ref[b]
        q = (x[:, 0:A_HEADS] * scale).reshape(nq, HEAD_DIM)
        knew = x[:, A_HEADS:2 * A_HEADS].reshape(nq, HEAD_DIM)
        vnew = x[:, 2 * A_HEADS:3 * A_HEADS].reshape(nq, HEAD_DIM)
        segs = [(flat(k1_ref[b, dense]), flat(v1_ref[b, dense]), mult_dense)]
        for kr, vr in ((k1_ref, v1_ref), (k2_ref, v2_ref)):
            for t in range(tn):
                segs.append((flat(kr[b, :, t]), flat(vr[b, :, t]), mult_res[t]))
        segs.append((knew, vnew, mult_new))

        scores = [_dot_nt(q, k) for k, _, _ in segs]
        m = None
        for s, (_, _, mult) in zip(scores, segs):
            ms = jnp.max(jnp.where(mult > 0.0, s, NEG), axis=-1, keepdims=True)
            m = ms if m is None else jnp.maximum(m, ms)
        probs = [jnp.where(mult > 0.0, jnp.exp(s - m), 0.0) * mult for s, (_, _, mult) in zip(scores, segs)]
        l = sum(jnp.sum(p, axis=-1, keepdims=True) for p in probs)
        acc = sum(_dot(p, v) for p, (_, v, _) in zip(probs, segs))
        o_ref[b] = (acc / l).reshape(tn, A_HEADS, HEAD_DIM)


def _attn_sample(x4, cache_k, cache_v):
    n, tn = x4.shape[0], x4.shape[1]
    w = cache_k.shape[1]
    assert w == A_WMAX and tn == 4
    views = []
    for cache in (cache_k, cache_v):
        views += [cache.reshape(n, w // 4, 4, A_HEADS, HEAD_DIM), cache.reshape(n, w // 16, 16, A_HEADS, HEAD_DIM)]
    k1, k2, v1, v2 = views
    nb = SAMPLE_ATTN_SEQS
    s1 = pl.BlockSpec((nb, A_KEYS, 4, A_HEADS, HEAD_DIM), lambda b: (b, w // 4 // A_KEYS - 1, 0, 0, 0))
    s2 = pl.BlockSpec((nb, A_KEYS, 4, A_HEADS, HEAD_DIM), lambda b: (b, 0, 0, 0, 0))
    return pl.pallas_call(
        functools.partial(_attn_sample_kernel, scale=HEAD_DIM ** -0.5),
        grid=(n // nb,),
        in_specs=[pl.BlockSpec((nb, tn, 3 * A_HEADS, HEAD_DIM), lambda b: (b, 0, 0, 0)), s1, s2, s1, s2],
        out_specs=pl.BlockSpec((nb, tn, A_HEADS, HEAD_DIM), lambda b: (b, 0, 0, 0)),
        out_shape=jax.ShapeDtypeStruct((n, tn, A_HEADS, HEAD_DIM), F32),
        compiler_params=_cparams("parallel"), name="attn_sample",
    )(x4, k1, k2, v1, v2)


def _unit_lower_inverse_minus_identity(low):
    c = low.shape[-1]
    base = min(16, c)
    row = lax.broadcasted_iota(jnp.int32, (1, c, c), 1)
    col = lax.broadcasted_iota(jnp.int32, (1, c, c), 2)
    nil = jnp.where((row // base) == (col // base), -low, 0.0)
    q = nil
    pw = nil
    for _ in range(int(math.log2(base)) - 1):
        pw = _bmm(pw, pw)
        q = q + pw + _bmm(q, pw)
    b = base
    while b < c:
        sib = ((row // (2 * b)) == (col // (2 * b))) & ((row // b) != (col // b))
        off = jnp.where(sib, low, 0.0)
        t = off + _bmm(q, off)
        q = q - t - _bmm(t, q)
        b *= 2
    return q


def _dn_prompt_kernel(x_ref, ba_ref, cw_ref, adt_ref, cinit_ref, sinit_ref, o_ref, sout_ref,
                      xbuf, s_ref, *, c, nb):
    ci = pl.program_id(0)
    halo = 8
    hk = B_HEADS * B_DK
    pairs = [(b, h) for b in range(nb) for h in range(B_HEADS)]

    @pl.when(ci == 0)
    def _():
        xbuf[:, 0:halo, :] = cinit_ref[...]
        s_ref[...] = sinit_ref[...].reshape(s_ref.shape)

    xbuf[:, halo:halo + c, :] = x_ref[...]
    cw = cw_ref[...]
    y = cw[0:1, :] * xbuf[:, pl.ds(halo - B_CONV + 1, c), :]
    for i in range(1, B_CONV):
        y = y + cw[i:i + 1, :] * xbuf[:, pl.ds(halo - B_CONV + 1 + i, c), :]
    xbuf[:, 0:halo, :] = xbuf[:, c:c + halo, :]
    cq = y * _sigmoid(y)

    ba = ba_ref[...]
    beta_all = _sigmoid(ba)
    g_all = -jnp.exp(adt_ref[0:1, :]) * _softplus(ba + adt_ref[1:2, :])
    row = lax.broadcasted_iota(jnp.int32, (c, c), 0)
    col = lax.broadcasted_iota(jnp.int32, (c, c), 1)
    tri_l = (row >= col).astype(F32)
    tri_u = (row <= col).astype(F32)
    gcum_col = [_dot_f32(tri_l, g_all[b]) for b in range(nb)]
    gcum_row = [_dot_tn_f32(g_all[b], tri_u) for b in range(nb)]

    def heads(off):
        return jnp.stack([cq[b, :, off + h * B_DK:off + (h + 1) * B_DK] for b, h in pairs])

    q, k, v = heads(0), heads(hk), heads(2 * hk)
    q = q * lax.rsqrt(jnp.sum(q * q, axis=-1, keepdims=True) + EPS) * (B_DK ** -0.5)
    k = k * lax.rsqrt(jnp.sum(k * k, axis=-1, keepdims=True) + EPS)
    bl, al = BA_LANE, BA_LANE + B_HEADS
    beta = jnp.stack([beta_all[b, :, bl + h:bl + h + 1] for b, h in pairs])
    gc = jnp.stack([gcum_col[b][:, al + h:al + h + 1] for b, h in pairs])
    gr = jnp.stack([gcum_row[b][al + h:al + h + 1, :] for b, h in pairs])
    decay = jnp.exp(jnp.where((row >= col)[None], gc - gr, NEG))
    kb = k * beta
    low = jnp.where((row > col)[None], _bmm_nt(kb, k) * decay, 0.0)
    qinv = _unit_lower_inverse_minus_identity(low)
    eg = jnp.exp(gc)
    rhs = jnp.concatenate([v * beta, kb * eg], axis=2)
    sol = rhs + _bmm(qinv, rhs)
    u = sol[:, :, 0:B_DK]
    w = sol[:, :, B_DK:2 * B_DK]
    attn = _bmm_nt(q, k) * decay
    glast = gc[:, c - 1:c, :]
    kd = k * jnp.exp(glast - gc)
    s = s_ref[...]
    v_new = u - _bmm(w, s)
    o = _bmm(q * eg, s) + _bmm(attn, v_new)
    s_ref[...] = s * jnp.exp(glast) + _bmm_tn(kd, v_new)
    for b in range(nb):
        o_ref[b] = jnp.concatenate([o[b * B_HEADS + h] for h in range(B_HEADS)], axis=1)

    @pl.when(ci == pl.num_programs(0) - 1)
    def _():
        sout_ref[...] = s_ref[...].reshape(sout_ref.shape)


def _dn_prompt(proj, conv_w, adt, conv_init, s_init):
    n, t, _ = proj.shape
    c = math.gcd(t, DN_CHUNK)
    width = 3 * B_HEADS * B_DK
    return pl.pallas_call(
        functools.partial(_dn_prompt_kernel, c=c, nb=n),
        grid=(t // c,),
        in_specs=[pl.BlockSpec((n, c, width), lambda i: (0, i, 1)),
                  pl.BlockSpec((n, c, LANES), lambda i: (0, i, proj.shape[2] // LANES - 1)),
                  pl.BlockSpec((B_CONV, width), lambda i: (0, 0)),
                  pl.BlockSpec((2, LANES), lambda i: (0, 0)),
                  pl.BlockSpec((n, 8, width), lambda i: (0, 0, 0)),
                  pl.BlockSpec((n, B_HEADS, B_DK, B_DK), lambda i: (0, 0, 0, 0))],
        out_specs=[pl.BlockSpec((n, c, B_HEADS * B_DK), lambda i: (0, i, 0)),
                   pl.BlockSpec((n, B_HEADS, B_DK, B_DK), lambda i: (0, 0, 0, 0))],
        out_shape=[jax.ShapeDtypeStruct((n, t, B_HEADS * B_DK), F32),
                   jax.ShapeDtypeStruct((n, B_HEADS, B_DK, B_DK), F32)],
        scratch_shapes=[pltpu.VMEM((n, 8 + c, width), F32), pltpu.VMEM((n * B_HEADS, B_DK, B_DK), F32)],
        compiler_params=_cparams("arbitrary"), name="deltanet_prompt",
    )(proj, proj, conv_w, adt, conv_init, s_init)


DN_SAMPLE_SEQS = 8


def _dn_sample_kernel(x_ref, bat_ref, cw_ref, adt_ref, cs_ref, sinit_ref, o_ref, s_ref):
    tn = 4
    hk = B_HEADS * B_DK
    seqs = range(DN_SAMPLE_SEQS)
    xp = jnp.concatenate([cs_ref[...], x_ref[...]], axis=1)
    cw = cw_ref[...]
    y = cw[0] * xp[:, 0:tn]
    for i in range(1, B_CONV):
        y = y + cw[i] * xp[:, i:i + tn]
    cq = y * _sigmoid(y)
    q = cq[:, :, 0:B_HEADS]
    k = cq[:, :, B_HEADS:2 * B_HEADS]
    v = cq[:, :, 2 * B_HEADS:3 * B_HEADS]
    q = q * lax.rsqrt(jnp.sum(q * q, axis=-1, keepdims=True) + EPS) * (B_DK ** -0.5)
    k = k * lax.rsqrt(jnp.sum(k * k, axis=-1, keepdims=True) + EPS)
    bat = bat_ref[...]
    beta = _sigmoid(bat[:, BA_LANE:BA_LANE + B_HEADS, :])
    a = jnp.exp(-jnp.exp(adt_ref[:, 0:1]) * _softplus(bat[:, BA_LANE + B_HEADS:LANES, :] + adt_ref[:, 1:2]))
    lane_head = lax.broadcasted_iota(jnp.int32, (B_HEADS, hk), 1) // B_DK
    head_mask = lane_head == lax.broadcasted_iota(jnp.int32, (B_HEADS, hk), 0)

    def block_diag(x):
        return jnp.where(head_mask, jnp.concatenate([x] * B_HEADS, axis=1), 0.0)

    s_ref[...] = sinit_ref[...]
    for t in range(tn):
        kbd = [block_diag(k[b, t]) for b in seqs]
        ks = [_dot(kbd[b], s_ref[b]) for b in seqs]
        w = [beta[b, :, t:t + 1] * (v[b, t] - a[b, :, t:t + 1] * ks[b]) for b in seqs]
        upd = [_dot_tn(kbd[b], w[b]) for b in seqs]
        for b in seqs:
            for h in range(B_HEADS):
                rows = slice(h * B_DK, (h + 1) * B_DK)
                s_ref[b, rows, :] = s_ref[b, rows, :] * a[b, h:h + 1, t:t + 1] + upd[b][rows, :]
        for b in seqs:
            o_ref[b, t] = _dot(block_diag(q[b, t]), s_ref[b])


def _dn_sample(x4, bat, conv_w, adt_col, conv_state, s_init):
    n = x4.shape[0]
    nb = DN_SAMPLE_SEQS
    rows = 3 * B_HEADS
    hk = B_HEADS * B_DK
    return pl.pallas_call(
        _dn_sample_kernel,
        grid=(n // nb,),
        in_specs=[pl.BlockSpec((nb, 4, rows, B_DK), lambda b: (b, 0, 1, 0)),
                  pl.BlockSpec((nb, LANES, 4), lambda b: (b, 0, 0)),
                  pl.BlockSpec((B_CONV, rows, B_DK), lambda b: (0, 0, 0)),
                  pl.BlockSpec((B_HEADS, 2), lambda b: (0, 0)),
                  pl.BlockSpec((nb, B_CONV - 1, rows, B_DK), lambda b: (b, 0, 0, 0)),
                  pl.BlockSpec((nb, hk, B_DK), lambda b: (b, 0, 0))],
        out_specs=[pl.BlockSpec((nb, 4, B_HEADS, B_DK), lambda b: (b, 0, 0, 0)),
                   pl.BlockSpec((nb, hk, B_DK), lambda b: (b, 0, 0))],
        out_shape=[jax.ShapeDtypeStruct((n, 4, B_HEADS, B_DK), F32),
                   jax.ShapeDtypeStruct((n, hk, B_DK), F32)],
        compiler_params=_cparams("parallel"), name="deltanet_sample",
    )(x4, bat, conv_w, adt_col, conv_state, s_init)


def _gla_kernel(q_ref, k_ref, v_ref, glr_ref, w2_ref, gb_ref, sinit_ref, o_ref, sout_ref,
                s_ref, *, c_real, c, dk, dv, nb):
    ci = pl.program_id(1)
    sub = min(GLA_SUB, c)

    @pl.when(ci == 0)
    def _():
        s_ref[...] = sinit_ref[...]

    def load(ref, b):
        x = ref[b]
        if c_real < c:
            x = jnp.concatenate([x, jnp.zeros((c - c_real, x.shape[1]), F32)], axis=0)
        return x

    row = lax.broadcasted_iota(jnp.int32, (c, c), 0)
    col = lax.broadcasted_iota(jnp.int32, (c, c), 1)
    tri = (row >= col).astype(F32)
    row_valid = lax.broadcasted_iota(jnp.int32, (c, 1), 0) < c_real
    sub_row = lax.broadcasted_iota(jnp.int32, (sub, 1), 0)
    sub_col = lax.broadcasted_iota(jnp.int32, (sub, c), 1)
    key_row = lax.broadcasted_iota(jnp.int32, (c, 1), 0)
    ones = jnp.ones((c, LANES), F32)
    starts = range(0, c, sub)
    seqs = []
    for b in range(nb):
        pre = _dot(load(glr_ref, b), w2_ref[...]) + gb_ref[...]
        la = jnp.where(row_valid, (jnp.minimum(pre, 0.0) - jnp.log(1.0 + jnp.exp(-jnp.abs(pre)))) / C_GATE_TAU, 0.0)
        gc = _dot_f32(tri, la)
        seqs.append((la, gc, load(q_ref, b) * (dk ** -0.5), load(k_ref, b), load(v_ref, b)))
    spans = [gc[lo:lo + 1, :] - gc[lo + sub - 1:lo + sub, :] for _, gc, _, _, _ in seqs for lo in starts]
    safe = jnp.max(functools.reduce(jnp.maximum, spans)) <= GLA_SAFE_SPAN

    def head(b, h):
        la, gc, q, k, v = seqs[b]
        ks = slice(h * dk, (h + 1) * dk)
        return la[:, ks], gc[:, ks], q[:, ks], k[:, ks], v[:, h * dv:(h + 1) * dv]

    for b in range(nb):
        outs = []
        for h in range(C_HEADS):
            la, gc, q, k, v = head(b, h)
            rows = []
            for lo in starts:
                ref_pt = gc[lo:lo + 1, :]
                qsc = q[lo:lo + sub, :] * jnp.exp(gc[lo:lo + sub, :] - ref_pt)
                limit = jnp.where(safe, lo + sub, lo)
                ksc = jnp.where(key_row < limit, k * jnp.exp(jnp.minimum(ref_pt - gc, GLA_SAFE_SPAN)), 0.0)
                rows.append(_dot_nt(qsc, ksc))
            attn = jnp.where(row >= col, jnp.concatenate(rows, axis=0), 0.0)
            s = s_ref[b, h]
            o = _dot(attn, v) + _dot(q * jnp.exp(gc), s)
            glast = gc[c - 1:c, :]
            kd = k * jnp.exp(glast - gc)
            gl_col = jnp.exp(_dot_tn_f32(la, ones))[:, 0:1]
            s_ref[b, h] = s * gl_col + _dot_tn(kd, v)
            outs.append(o[0:c_real, :])
        o_ref[b] = jnp.concatenate(outs, axis=1)

    @pl.when(jnp.logical_not(safe))
    def _():
        for b in range(nb):
            fixes = []
            for h in range(C_HEADS):
                _, gc, q, k, v = head(b, h)
                rows = []
                for lo in starts:
                    qb = q[lo:lo + sub, :]
                    gblk = gc[lo:lo + sub, :]
                    blk = jnp.zeros((sub, c), F32)
                    for jj in range(sub):
                        j = lo + jj
                        e = jnp.exp(jnp.minimum(gblk - gc[j:j + 1, :], 0.0))
                        a = jnp.sum(qb * k[j:j + 1, :] * e, axis=-1, keepdims=True)
                        blk = jnp.where((sub_col == j) & (sub_row >= jj), a, blk)
                    rows.append(blk)
                fixes.append(_dot(jnp.concatenate(rows, axis=0), v)[0:c_real, :])
            o_ref[b] += jnp.concatenate(fixes, axis=1)

    @pl.when(ci == pl.num_programs(1) - 1)
    def _():
        sout_ref[...] = s_ref[...]


def _gla(proj, w2, gb, s_init, nb):
    n, t, _ = proj.shape
    dk, dv = s_init.shape[2], s_init.shape[3]
    c_real = math.gcd(t, GLA_CHUNK)
    c = max(c_real, 8)
    hk, hv = C_HEADS * dk, C_HEADS * dv
    return pl.pallas_call(
        functools.partial(_gla_kernel, c_real=c_real, c=c, dk=dk, dv=dv, nb=nb),
        grid=(n // nb, t // c_real),
        in_specs=[pl.BlockSpec((nb, c_real, hk), lambda b, i: (b, i, 0)),
                  pl.BlockSpec((nb, c_real, hk), lambda b, i: (b, i, 1)),
                  pl.BlockSpec((nb, c_real, hv), lambda b, i: (b, i, 2 * hk // hv)),
                  pl.BlockSpec((nb, c_real, LANES), lambda b, i: (b, i, (2 * hk + 2 * hv) // LANES)),
                  pl.BlockSpec((LANES, hk), lambda b, i: (0, 0)),
                  pl.BlockSpec((1, hk), lambda b, i: (0, 0)),
                  pl.BlockSpec((nb, C_HEADS, dk, dv), lambda b, i: (b, 0, 0, 0))],
        out_specs=[pl.BlockSpec((nb, c_real, hv), lambda b, i: (b, i, 0)),
                   pl.BlockSpec((nb, C_HEADS, dk, dv), lambda b, i: (b, 0, 0, 0))],
        out_shape=[jax.ShapeDtypeStruct((n, t, hv), F32),
                   jax.ShapeDtypeStruct((n, C_HEADS, dk, dv), F32)],
        scratch_shapes=[pltpu.VMEM((nb, C_HEADS, dk, dv), F32)],
        compiler_params=_cparams("parallel", "arbitrary"), name="gla",
    )(proj, proj, proj, proj, w2, gb, s_init)


def _gated_headnorm(o, z, nw, heads, width):
    parts = []
    for h in range(heads):
        zz = z[:, h * width:(h + 1) * width]
        parts.append(_rms(o[:, h * width:(h + 1) * width], nw) * (zz * _sigmoid(zz)))
    return jnp.concatenate(parts, axis=1)


def _row_parts(tm):
    tp = tm // MIX_PARTS if tm % (8 * MIX_PARTS) == 0 else tm
    return [slice(lo, lo + tp) for lo in range(0, tm, tp)]


def _mix_even_kernel(oa_ref, ob_ref, z_ref, nw_ref, w_ref, r_ref, g_ref, out_ref):
    ka = A_HEADS * HEAD_DIM
    for rows in _row_parts(out_ref.shape[0]):
        obn = _gated_headnorm(ob_ref[rows, :], z_ref[rows, :], nw_ref[...], B_HEADS, B_DK)
        m = _dot(oa_ref[rows, :], w_ref[0:ka, :]) + _dot(obn, w_ref[ka:, :])
        out_ref[rows, :] = r_ref[rows, :] + _rms(m, g_ref[...])


def _mix_even(o_a, o_b, proj, nw, w_out, r, g_post, tm):
    m, d = r.shape
    ka = o_a.shape[1]
    kb = o_b.shape[1]
    z_block = (2 * 3 * A_HEADS * HEAD_DIM) // kb
    return pl.pallas_call(
        _mix_even_kernel, grid=(m // tm,),
        in_specs=[pl.BlockSpec((tm, ka), lambda i: (i, 0)),
                  pl.BlockSpec((tm, kb), lambda i: (i, 0)),
                  pl.BlockSpec((tm, kb), lambda i: (i, z_block)),
                  pl.BlockSpec((1, B_DK), lambda i: (0, 0)),
                  pl.BlockSpec((ka + kb, d), lambda i: (0, 0)),
                  pl.BlockSpec((tm, d), lambda i: (i, 0)),
                  pl.BlockSpec((1, d), lambda i: (0, 0))],
        out_specs=pl.BlockSpec((tm, d), lambda i: (i, 0)),
        out_shape=jax.ShapeDtypeStruct((m, d), F32),
        compiler_params=_cparams("parallel"), name="mix_even",
    )(o_a, o_b, proj, nw.reshape(1, -1), w_out, r, g_post.reshape(1, d))


def _mix_odd_kernel(o_ref, z_ref, nw_ref, w_ref, r_ref, g_ref, out_ref, *, dv):
    for rows in _row_parts(out_ref.shape[0]):
        on = _gated_headnorm(o_ref[rows, :], z_ref[rows, :], nw_ref[...], C_HEADS, dv)
        out_ref[rows, :] = r_ref[rows, :] + _rms(_dot(on, w_ref[...]), g_ref[...])


def _mix_odd(o, proj, nw, w_out, r, g_post, tm):
    m, d = r.shape
    kv = o.shape[1]
    return pl.pallas_call(
        functools.partial(_mix_odd_kernel, dv=kv // C_HEADS), grid=(m // tm,),
        in_specs=[pl.BlockSpec((tm, kv), lambda i: (i, 0)),
                  pl.BlockSpec((tm, kv), lambda i: (i, 2)),
                  pl.BlockSpec((1, kv // C_HEADS), lambda i: (0, 0)),
                  pl.BlockSpec((kv, d), lambda i: (0, 0)),
                  pl.BlockSpec((tm, d), lambda i: (i, 0)),
                  pl.BlockSpec((1, d), lambda i: (0, 0))],
        out_specs=pl.BlockSpec((tm, d), lambda i: (i, 0)),
        out_shape=jax.ShapeDtypeStruct((m, d), F32),
        compiler_params=_cparams("parallel"), name="mix_odd",
    )(o, proj, nw.reshape(1, -1), w_out, r, g_post.reshape(1, d))


def _ffn_kernel(x_ref, gpre_ref, wg_ref, wv_ref, cwg_ref, cwv_ref, wd_ref, gpost_ref, ig_ref, iv_ref,
                o_ref, tg_ref, tv_ref, h_ref, ug, uv, *carry, tm, halo, shift):
    i = pl.program_id(1)
    j = pl.program_id(2)

    @pl.when(j == 0)
    def _():
        h_ref[...] = _rms(x_ref[0], gpre_ref[...]).astype(BF16)
        o_ref[0] = jnp.zeros_like(o_ref[0])

    if carry:
        cg, cv = carry

        @pl.when(i == 0)
        def _():
            ug[0:halo, :] = ig_ref[0]
            uv[0:halo, :] = iv_ref[0]

        @pl.when(i > 0)
        def _():
            ug[0:halo, :] = cg[j]
            uv[0:halo, :] = cv[j]
    else:
        ug[0:halo, :] = ig_ref[0]
        uv[0:halo, :] = iv_ref[0]

    tf = ug.shape[1]
    sub = min(FFN_SUB, tf)
    cols = [slice(s * sub, (s + 1) * sub) for s in range(tf // sub)]
    for cs in cols:
        ug[halo:halo + tm, cs] = jnp.dot(h_ref[...], wg_ref[:, cs], preferred_element_type=F32)
        uv[halo:halo + tm, cs] = jnp.dot(h_ref[...], wv_ref[:, cs], preferred_element_type=F32)

    def conv(u, cw_ref, cs):
        y = cw_ref[0:1, cs] * u[pl.ds(halo - (FFN_CONV - 1) * shift, tm), cs]
        for tap in range(1, FFN_CONV):
            y = y + cw_ref[tap:tap + 1, cs] * u[pl.ds(halo - (FFN_CONV - 1 - tap) * shift, tm), cs]
        return y

    acc = None
    for cs in cols:
        act = _gelu_tanh(conv(ug, cwg_ref, cs)) * conv(uv, cwv_ref, cs)
        part = _dot(act, wd_ref[cs, :])
        acc = part if acc is None else acc + part
    tail_g = ug[tm:tm + halo, :]
    tail_v = uv[tm:tm + halo, :]
    tg_ref[0, 0] = tail_g
    tv_ref[0, 0] = tail_v
    if carry:
        cg[j] = tail_g
        cv[j] = tail_v
    o_ref[0] += acc

    @pl.when(j == pl.num_programs(2) - 1)
    def _():
        o_ref[0] = x_ref[0] + _rms(o_ref[0], gpost_ref[...])


def _ffn(x, g_pre, w_up, conv_w, w_down, g_post, init, layer, tm, tf, shift):
    n, t, d = x.shape
    dff = w_down.shape[1]
    halo = init.shape[1]
    ni, nj = t // tm, dff // tf
    scratch = [pltpu.VMEM((tm, d), BF16), pltpu.VMEM((halo + tm, tf), F32), pltpu.VMEM((halo + tm, tf), F32)]
    if ni > 1:
        scratch += [pltpu.VMEM((nj, halo, tf), F32), pltpu.VMEM((nj, halo, tf), F32)]
    out, tail_g, tail_v = pl.pallas_call(
        functools.partial(_ffn_kernel, tm=tm, halo=halo, shift=shift),
        grid=(n, ni, nj),
        in_specs=[pl.BlockSpec((1, tm, d), lambda b, i, j: (b, i, 0)),
                  pl.BlockSpec((1, d), lambda b, i, j: (0, 0)),
                  pl.BlockSpec((None, d, tf), lambda b, i, j: (layer, 0, j)),
                  pl.BlockSpec((None, d, tf), lambda b, i, j: (layer, 0, j + nj)),
                  pl.BlockSpec((None, FFN_CONV, tf), lambda b, i, j: (layer, 0, j)),
                  pl.BlockSpec((None, FFN_CONV, tf), lambda b, i, j: (layer, 0, j + nj)),
                  pl.BlockSpec((None, tf, d), lambda b, i, j: (layer, j, 0)),
                  pl.BlockSpec((1, d), lambda b, i, j: (0, 0)),
                  pl.BlockSpec((1, halo, tf), lambda b, i, j: (b, 0, j)),
                  pl.BlockSpec((1, halo, tf), lambda b, i, j: (b, 0, j + nj))],
        out_specs=[pl.BlockSpec((1, tm, d), lambda b, i, j: (b, i, 0), pipeline_mode=pl.Buffered(1)),
                   pl.BlockSpec((1, 1, halo, tf), lambda b, i, j: (b, i, 0, j)),
                   pl.BlockSpec((1, 1, halo, tf), lambda b, i, j: (b, i, 0, j))],
        out_shape=[jax.ShapeDtypeStruct((n, t, d), F32),
                   jax.ShapeDtypeStruct((n, ni, halo, dff), F32),
                   jax.ShapeDtypeStruct((n, ni, halo, dff), F32)],
        scratch_shapes=scratch,
        compiler_params=_cparams("parallel", "arbitrary", "arbitrary"), name="conv_ffn",
    )(x, g_pre.reshape(1, d), w_up, w_up, conv_w, conv_w, w_down, g_post.reshape(1, d), init, init)
    return out, jnp.concatenate([tail_g[:, ni - 1], tail_v[:, ni - 1]], axis=-1)


def _ple_kernel(r_ref, p_ref, wp_ref, wg_ref, o_ref):
    r = r_ref[...]
    o_ref[...] = r + _dot(p_ref[...], wp_ref[...]) * _sigmoid(_dot(r, wg_ref[...]))


def _ple(r, p, w_proj, w_gate, layer, tm):
    m, d = r.shape
    pd = p.shape[1]
    return pl.pallas_call(
        _ple_kernel, grid=(m // tm,),
        in_specs=[pl.BlockSpec((tm, d), lambda i: (i, 0)),
                  pl.BlockSpec((tm, pd), lambda i: (i, 0)),
                  pl.BlockSpec((None, pd, d), lambda i: (layer, 0, 0)),
                  pl.BlockSpec((None, d, d), lambda i: (layer, 0, 0))],
        out_specs=pl.BlockSpec((tm, d), lambda i: (i, 0)),
        out_shape=jax.ShapeDtypeStruct((m, d), F32),
        compiler_params=_cparams("parallel"), name="ple",
    )(r, p, w_proj, w_gate)


def _reorder_cols_kernel(w_ref, o_ref, *, spans):
    x = w_ref[...]
    col = 0
    for a, b in spans:
        o_ref[:, col:col + b - a] = x[:, a:b].astype(BF16)
        col += b - a
    if col < o_ref.shape[1]:
        o_ref[:, col:] = jnp.zeros((x.shape[0], o_ref.shape[1] - col), BF16)


def _reorder_cols(w, spans, tn):
    k, n_in = w.shape
    spans = [(a, n_in if b is None else b) for a, b in spans]
    width = sum(b - a for a, b in spans)
    n_out = width + (-width % tn)
    tr = WEIGHT_PREP_ROWS
    return pl.pallas_call(
        functools.partial(_reorder_cols_kernel, spans=spans),
        grid=(k // tr,),
        in_specs=[pl.BlockSpec((tr, n_in), lambda i: (i, 0))],
        out_specs=pl.BlockSpec((tr, n_out), lambda i: (i, 0)),
        out_shape=jax.ShapeDtypeStruct((k, n_out), BF16),
        compiler_params=_cparams("parallel"), name="weight_prep",
    )(w)


def _prep_weights(w_in_even, w_out_even, w_in_odd, w_out_odd, gla_gate_w2, ffn_w_up, ffn_w_down,
                  ple_w_proj, ple_w_gate, dn_a_log, dn_dt_bias):
    assert (3 * A_HEADS * HEAD_DIM + 3 * B_HEADS * B_DK) % PROJ0_TN == 0
    w0 = _reorder_cols(w_in_even[0], [(0, None)], PROJ0_TN)
    assert w0.shape[1] - w_in_even.shape[2] >= LANES - 2 * B_HEADS
    d = w_in_odd.shape[1]
    qkv = 2 * (d // 2) + d
    w1 = _reorder_cols(w_in_odd[0], [(0, qkv), (qkv + C_GATE_RANK, None), (qkv, qkv + C_GATE_RANK)], PROJ_TN)
    w2 = jnp.pad(gla_gate_w2[0], ((0, LANES - C_GATE_RANK), (0, 0))).astype(BF16)
    adt_row = jnp.pad(jnp.stack([dn_a_log[0], dn_dt_bias[0]]), ((0, 0), (BA_LANE + B_HEADS, 0)))
    adt_col = jnp.stack([dn_a_log[0], dn_dt_bias[0]], axis=1)
    return dict(w0=w0, w1=w1, w2=w2, adt_row=adt_row, adt_col=adt_col,
                w_out_even=w_out_even[0].astype(BF16), w_out_odd=w_out_odd[0].astype(BF16),
                w_up=ffn_w_up.astype(BF16), w_down=ffn_w_down.astype(BF16),
                w_proj=ple_w_proj.astype(BF16), w_gate=ple_w_gate.astype(BF16))


def _layer_tail(r, p, layer, wt, prm, ffn_init, n, t, tm):
    d = r.shape[1]
    r, tail = _ffn(r.reshape(n, t, d), prm["norm_ffn_pre"][layer], wt["w_up"], prm["ffn_conv_w"],
                   wt["w_down"], prm["norm_ffn_post"][layer], ffn_init, layer, PROMPT_FFN_TM, PROMPT_FFN_TF, 1)
    r = _ple(r.reshape(n * t, d), p, wt["w_proj"], wt["w_gate"], layer, tm)
    return r, tail


def _prompt_group(x, p, wt, prm):
    n, t, d = x.shape
    m = n * t
    tm = MIX_TM
    dff2 = prm["ffn_conv_w"].shape[-1]
    r = x.reshape(m, d)
    proj = _rms_matmul(r, prm["norm_mix_pre"][0], wt["w0"], PROMPT_PROJ_TM, PROJ0_TN, lead=2 * B_HEADS)
    pv = proj.reshape(n, t, -1)
    o_a = _attn_prompt(pv, n, t).reshape(m, -1)
    width = 3 * B_HEADS * B_DK
    o_b, dn_state = _dn_prompt(pv, prm["dn_conv_w"][0], wt["adt_row"], jnp.zeros((n, 8, width), F32),
                               jnp.zeros((n, B_HEADS, B_DK, B_DK), F32))
    r = _mix_even(o_a, o_b.reshape(m, -1), proj, prm["dn_norm_w"][0], wt["w_out_even"], r,
                  prm["norm_mix_post"][0], tm)
    keep = min(A_WMAX, t)
    hd = A_HEADS * HEAD_DIM
    win_k = pv[:, t - keep:, hd:2 * hd].reshape(n, keep, A_HEADS, HEAD_DIM)
    win_v = pv[:, t - keep:, 2 * hd:3 * hd].reshape(n, keep, A_HEADS, HEAD_DIM)
    dn_conv = pv[:, t - (B_CONV - 1):, width:2 * width]
    ffn_zero = jnp.zeros((n, 8, dff2), F32)
    r, tail0 = _layer_tail(r, p[0].reshape(m, -1), 0, wt, prm, ffn_zero, n, t, tm)
    proj1 = _rms_matmul(r, prm["norm_mix_pre"][1], wt["w1"], PROMPT_PROJ_TM, PROJ_TN)
    o_c, gla_state = _gla(proj1.reshape(n, t, -1), wt["w2"], prm["gla_gate_b"][0].reshape(1, -1),
                          jnp.zeros((n, C_HEADS, d // (2 * C_HEADS), d // C_HEADS), F32), n)
    r = _mix_odd(o_c.reshape(m, -1), proj1, prm["gla_norm_w"][0], wt["w_out_odd"], r, prm["norm_mix_post"][1], tm)
    r, tail1 = _layer_tail(r, p[1].reshape(m, -1), 1, wt, prm, ffn_zero, n, t, tm)
    ffn_conv = jnp.stack([tail0[:, 8 - (FFN_CONV - 1):], tail1[:, 8 - (FFN_CONV - 1):]])
    return (r.reshape(n, t, d), win_k[None], win_v[None], dn_conv[None], dn_state[None], gla_state[None], ffn_conv)


def _sample_group(x, p, cache_k, cache_v, dn_conv_state, dn_state, gla_state, ffn_state, wt, prm):
    n, t, d = x.shape
    m = n * t
    tm = MIX_TM
    r = x.reshape(m, d)

    def time_major(a):
        return a.reshape(n, t, -1).transpose(1, 0, 2).reshape(1, m, -1)

    def seq_major(a):
        return a.reshape(t, n, -1).transpose(1, 0, 2).reshape(m, -1)

    def ffn_layer(r, layer):
        init = ffn_state[layer].transpose(1, 0, 2).reshape(1, (FFN_CONV - 1) * n, -1)
        rt, tail = _ffn(time_major(r), prm["norm_ffn_pre"][layer], wt["w_up"], prm["ffn_conv_w"],
                        wt["w_down"], prm["norm_ffn_post"][layer], init, layer, m, SAMPLE_FFN_TF, n)
        r = _ple(seq_major(rt), p[layer].reshape(m, -1), wt["w_proj"], wt["w_gate"], layer, tm)
        return r, tail.reshape(FFN_CONV - 1, n, -1).transpose(1, 0, 2)

    proj = _rms_matmul(r, prm["norm_mix_pre"][0], wt["w0"], m, PROJ0_TN, lead=2 * B_HEADS)
    width = 3 * B_HEADS * B_DK
    heads = proj.reshape(n, t, -1, HEAD_DIM)
    o_a = _attn_sample(heads, cache_k[0], cache_v[0])
    bat = heads[:, :, heads.shape[2] - 1].transpose(0, 2, 1)
    o_b, dn_new = _dn_sample(heads, bat, prm["dn_conv_w"][0].reshape(B_CONV, -1, B_DK), wt["adt_col"],
                             dn_conv_state[0].reshape(n, B_CONV - 1, -1, B_DK),
                             dn_state[0].reshape(n, B_HEADS * B_DK, B_DK))
    r = _mix_even(o_a.reshape(m, -1), o_b.reshape(m, -1), proj, prm["dn_norm_w"][0], wt["w_out_even"], r,
                  prm["norm_mix_post"][0], tm)
    win_k = heads[:, :, A_HEADS:2 * A_HEADS]
    win_v = heads[:, :, 2 * A_HEADS:3 * A_HEADS]
    dn_conv = heads[:, t - (B_CONV - 1):, 3 * A_HEADS:3 * (A_HEADS + B_HEADS)].reshape(n, B_CONV - 1, width)
    r, tail0 = ffn_layer(r, 0)
    proj1 = _rms_matmul(r, prm["norm_mix_pre"][1], wt["w1"], m, PROJ_TN)
    o_c, gla_new = _gla(proj1.reshape(n, t, -1), wt["w2"], prm["gla_gate_b"][0].reshape(1, -1), gla_state[0],
                        SAMPLE_GLA_SEQS)
    r = _mix_odd(o_c.reshape(m, -1), proj1, prm["gla_norm_w"][0], wt["w_out_odd"], r, prm["norm_mix_post"][1], tm)
    r, tail1 = ffn_layer(r, 1)
    return (r.reshape(n, t, d), win_k[None], win_v[None], dn_conv[None],
            dn_new.reshape(dn_state.shape), gla_new[None], jnp.stack([tail0, tail1]))


def kernel(x_prompt, x_sample, cache_win_k, cache_win_v, state_dn_conv, state_dn, state_gla, state_ffn_conv, p_prompt, p_sample, norm_mix_pre, norm_mix_post, norm_ffn_pre, norm_ffn_post, w_in_even, w_out_even, dn_conv_w, dn_a_log, dn_dt_bias, dn_norm_w, w_in_odd, gla_gate_w2, gla_gate_b, gla_norm_w, w_out_odd, ffn_w_up, ffn_conv_w, ffn_w_down, ple_w_proj, ple_w_gate):
    wt = _prep_weights(w_in_even, w_out_even, w_in_odd, w_out_odd, gla_gate_w2, ffn_w_up, ffn_w_down,
                       ple_w_proj, ple_w_gate, dn_a_log, dn_dt_bias)
    prm = dict(norm_mix_pre=norm_mix_pre, norm_mix_post=norm_mix_post, norm_ffn_pre=norm_ffn_pre,
               norm_ffn_post=norm_ffn_post, dn_conv_w=dn_conv_w, dn_norm_w=dn_norm_w, gla_gate_b=gla_gate_b,
               gla_norm_w=gla_norm_w, ffn_conv_w=ffn_conv_w)
    yp, kp, vp, dcp, dsp, gsp, fcp = _prompt_group(x_prompt, p_prompt, wt, prm)
    ys, ks, vs, dcs, dss, gss, fcs = _sample_group(x_sample, p_sample, cache_win_k, cache_win_v, state_dn_conv,
                                                   state_dn, state_gla, state_ffn_conv, wt, prm)
    return (yp, ys, kp, vp, dcp, dsp, gsp, fcp, ks, vs, dcs, dss, gss, fcs)
```

```python
import functools
import math

import jax
import jax.numpy as jnp
from jax import lax
from jax.experimental import pallas as pl
from jax.experimental.pallas import tpu as pltpu

F32 = jnp.float32
BF16 = jnp.bfloat16
EPS = 1e-6
NEG = -1e30
HIGHEST = lax.Precision.HIGHEST

LANES = 128
V7X_VMEM_LIMIT_BYTES = 60 * 2**20

HEAD_DIM = 128
A_HEADS = 8
A_KEYS = 128
A_DILATIONS = (1, 4, 16)
A_WMAX = 2048
B_HEADS = 8
B_DK = 128
B_CONV = 4
C_HEADS = 4
C_GATE_RANK = 16
C_GATE_TAU = 16.0
FFN_CONV = 3
DN_CHUNK = 64
GLA_CHUNK = 64
GLA_SUB = 16
GLA_SAFE_SPAN = 60.0
PROJ_TN = 1280
PROJ0_TN = 1536
BA_LANE = 128 - 2 * 8
PROMPT_PROJ_TM = 1024
PROMPT_FFN_TM = 1024
PROMPT_FFN_TF = 512
FFN_SUB = 512
SAMPLE_FFN_TF = 512
WEIGHT_PREP_ROWS = 256
MIX_TM = 512
MIX_PARTS = 2
SAMPLE_GLA_SEQS = 4
SAMPLE_ATTN_SEQS = 2


def _cparams(*sem):
    return pltpu.CompilerParams(dimension_semantics=sem, vmem_limit_bytes=V7X_VMEM_LIMIT_BYTES)


def _dot(a, b):
    return jnp.dot(a.astype(BF16), b.astype(BF16), preferred_element_type=F32)


def _dot_nt(a, b):
    return lax.dot_general(a.astype(BF16), b.astype(BF16), (((1,), (1,)), ((), ())),
                           preferred_element_type=F32)


def _dot_tn(a, b):
    return lax.dot_general(a.astype(BF16), b.astype(BF16), (((0,), (0,)), ((), ())),
                           preferred_element_type=F32)


def _bmm(a, b):
    return lax.dot_general(a.astype(BF16), b.astype(BF16), (((2,), (1,)), ((0,), (0,))),
                           preferred_element_type=F32)


def _bmm_nt(a, b):
    return lax.dot_general(a.astype(BF16), b.astype(BF16), (((2,), (2,)), ((0,), (0,))),
                           preferred_element_type=F32)


def _bmm_tn(a, b):
    return lax.dot_general(a.astype(BF16), b.astype(BF16), (((1,), (1,)), ((0,), (0,))),
                           preferred_element_type=F32)


def _dot_f32(a, b):
    return jnp.dot(a, b, precision=HIGHEST, preferred_element_type=F32)


def _dot_tn_f32(a, b):
    return lax.dot_general(a, b, (((0,), (0,)), ((), ())), precision=HIGHEST,
                           preferred_element_type=F32)


def _rms(x, g):
    return x * lax.rsqrt(jnp.mean(x * x, axis=-1, keepdims=True) + EPS) * g


def _sigmoid(x):
    return 1.0 / (1.0 + jnp.exp(-x))


def _softplus(x):
    return jnp.maximum(x, 0.0) + jnp.log(1.0 + jnp.exp(-jnp.abs(x)))


def _gelu_tanh(x):
    return 0.5 * x * (1.0 + jnp.tanh(math.sqrt(2.0 / math.pi) * (x + 0.044715 * (x * x * x))))


def _rms_matmul_kernel(x_ref, g_ref, w_ref, o_ref, h_ref):
    @pl.when(pl.program_id(1) == 0)
    def _():
        h_ref[...] = _rms(x_ref[...], g_ref[...]).astype(BF16)

    o_ref[...] = jnp.dot(h_ref[...], w_ref[...], preferred_element_type=F32)


def _rms_matmul(x, g, w, tm, tn):
    m, k = x.shape
    n = w.shape[1]
    return pl.pallas_call(
        _rms_matmul_kernel,
        grid=(m // tm, n // tn),
        in_specs=[pl.BlockSpec((tm, k), lambda i, j: (i, 0)),
                  pl.BlockSpec((1, k), lambda i, j: (0, 0)),
                  pl.BlockSpec((k, tn), lambda i, j: (0, j))],
        out_specs=pl.BlockSpec((tm, tn), lambda i, j: (i, j)),
        out_shape=jax.ShapeDtypeStruct((m, n), F32),
        scratch_shapes=[pltpu.VMEM((tm, k), BF16)],
        compiler_params=_cparams("parallel", "arbitrary"),
        name="rms_matmul",
    )(x, g.reshape(1, k), w)


ATTN_SB = A_WMAX


def _attn_prompt_kernel(q_ref, kc_ref, kp_ref, vc_ref, vp_ref, o_ref, o_s, l_s, *, scale):
    has_prev = pl.program_id(2) > 0
    row = lax.broadcasted_iota(jnp.int32, (A_KEYS, 2 * A_KEYS), 0)
    col = lax.broadcasted_iota(jnp.int32, (A_KEYS, 2 * A_KEYS), 1)
    cur_ok = (col >= A_KEYS) & (col - A_KEYS <= row)
    prev_ok = (col < A_KEYS) & (col >= row)

    def tile(ref, start, d):
        if d == 1:
            return ref[0, pl.ds(start, A_KEYS), :]
        return ref[0, pl.ds(start, A_KEYS, stride=d), :]

    for g, d in enumerate(A_DILATIONS):
        span = A_KEYS * d
        for r in range(d):
            for j in range(ATTN_SB // span):
                start = r + span * j
                q = tile(q_ref, start, d) * scale
                kc, vc = tile(kc_ref, start, d).astype(BF16), tile(vc_ref, start, d).astype(BF16)
                if j > 0:
                    ok = cur_ok | prev_ok
                else:
                    kp = tile(kp_ref, ATTN_SB - span + r, d).astype(BF16)
                    vp = tile(vp_ref, ATTN_SB - span + r, d).astype(BF16)
                    ok = cur_ok | (prev_ok & has_prev)
                s = _dot_nt(q, jnp.concatenate([kp, kc], axis=0))
                s = jnp.where(ok, s, NEG)
                m = jnp.max(s, axis=-1, keepdims=True)
                p = jnp.exp(s - m)
                l = jnp.sum(p, axis=-1, keepdims=True)
                o = _dot(p, jnp.concatenate([vp, vc], axis=0)) / l
                kp, vp = kc, vc
                lse = jnp.broadcast_to(m + jnp.log(l), (A_KEYS, HEAD_DIM))
                if d == 1:
                    o_s[g, pl.ds(start, A_KEYS), :] = o
                    l_s[g, pl.ds(start, A_KEYS), :] = lse
                else:
                    o_s[g, pl.ds(start, A_KEYS, stride=d), :] = o
                    l_s[g, pl.ds(start, A_KEYS, stride=d), :] = lse
    la, lb, lc = l_s[0], l_s[1], l_s[2]
    m = jnp.maximum(jnp.maximum(la, lb), lc)
    wa, wb, wc = jnp.exp(la - m), jnp.exp(lb - m), jnp.exp(lc - m)
    o_ref[0] = (wa * o_s[0] + wb * o_s[1] + wc * o_s[2]) / (wa + wb + wc)


def _attn_prompt(proj, n, t):
    assert t % ATTN_SB == 0 and len(A_DILATIONS) == 3
    cur = lambda off: (lambda b, h, i: (b, i, off + h))
    prev = lambda off: (lambda b, h, i: (b, jnp.maximum(i - 1, 0), off + h))
    blk = (1, ATTN_SB, HEAD_DIM)
    return pl.pallas_call(
        functools.partial(_attn_prompt_kernel, scale=HEAD_DIM ** -0.5),
        grid=(n, A_HEADS, t // ATTN_SB),
        in_specs=[pl.BlockSpec(blk, cur(0)),
                  pl.BlockSpec(blk, cur(A_HEADS)), pl.BlockSpec(blk, prev(A_HEADS)),
                  pl.BlockSpec(blk, cur(2 * A_HEADS)), pl.BlockSpec(blk, prev(2 * A_HEADS))],
        out_specs=pl.BlockSpec(blk, lambda b, h, i: (b, i, h)),
        out_shape=jax.ShapeDtypeStruct((n, t, A_HEADS * HEAD_DIM), F32),
        scratch_shapes=[pltpu.VMEM((3, ATTN_SB, HEAD_DIM), F32), pltpu.VMEM((3, ATTN_SB, HEAD_DIM), F32)],
        compiler_params=_cparams("parallel", "parallel", "arbitrary"), name="attn_prompt",
    )(proj, proj, proj, proj, proj)


def _attn_sample_kernel(x_ref, k1_ref, k2_ref, v1_ref, v2_ref, o_ref, *, scale):
    tn = 4
    nq = tn * A_HEADS
    nk = A_KEYS * A_HEADS

    def grid_masks(cols):
        qrow = lax.broadcasted_iota(jnp.int32, (nq, cols), 0)
        kcol = lax.broadcasted_iota(jnp.int32, (nq, cols), 1)
        return (qrow % A_HEADS) == (kcol % A_HEADS), qrow // A_HEADS, kcol // A_HEADS

    same_head, qt, key = grid_masks(nk)
    mult_dense = jnp.where(same_head & (key >= qt), 1.0, 0.0)
    mult_res = [jnp.where(same_head & (qt == t), 1.0, 0.0) for t in range(tn)]
    same_head, qt, kt = grid_masks(nq)
    mult_new = jnp.where(same_head & (kt == qt), float(len(A_DILATIONS)), jnp.where(same_head & (kt < qt), 1.0, 0.0))
    flat = lambda a: a.reshape(nk, HEAD_DIM)
    dense = slice(A_KEYS - A_KEYS // 4, A_KEYS)

    for b in range(x_ref.shape[0]):
        x = x_ref[b]
        q = (x[:, 0:A_HEADS] * scale).reshape(nq, HEAD_DIM)
        knew = x[:, A_HEADS:2 * A_HEADS].reshape(nq, HEAD_DIM)
        vnew = x[:, 2 * A_HEADS:3 * A_HEADS].reshape(nq, HEAD_DIM)
        segs = [(flat(k1_ref[b, dense]), flat(v1_ref[b, dense]), mult_dense)]
        for kr, vr in ((k1_ref, v1_ref), (k2_ref, v2_ref)):
            for t in range(tn):
                segs.append((flat(kr[b, :, t]), flat(vr[b, :, t]), mult_res[t]))
        segs.append((knew, vnew, mult_new))

        scores = [_dot_nt(q, k) for k, _, _ in segs]
        m = None
        for s, (_, _, mult) in zip(scores, segs):
            ms = jnp.max(jnp.where(mult > 0.0, s, NEG), axis=-1, keepdims=True)
            m = ms if m is None else jnp.maximum(m, ms)
        probs = [jnp.where(mult > 0.0, jnp.exp(s - m), 0.0) * mult for s, (_, _, mult) in zip(scores, segs)]
        l = sum(jnp.sum(p, axis=-1, keepdims=True) for p in probs)
        acc = sum(_dot(p, v) for p, (_, v, _) in zip(probs, segs))
        o_ref[b] = (acc / l).reshape(tn, A_HEADS, HEAD_DIM)


def _attn_sample(x4, cache_k, cache_v):
    n, tn = x4.shape[0], x4.shape[1]
    w = cache_k.shape[1]
    assert w == A_WMAX and tn == 4
    views = []
    for cache in (cache_k, cache_v):
        views += [cache.reshape(n, w // 4, 4, A_HEADS, HEAD_DIM), cache.reshape(n, w // 16, 16, A_HEADS, HEAD_DIM)]
    k1, k2, v1, v2 = views
    nb = SAMPLE_ATTN_SEQS
    s1 = pl.BlockSpec((nb, A_KEYS, 4, A_HEADS, HEAD_DIM), lambda b: (b, w // 4 // A_KEYS - 1, 0, 0, 0))
    s2 = pl.BlockSpec((nb, A_KEYS, 4, A_HEADS, HEAD_DIM), lambda b: (b, 0, 0, 0, 0))
    return pl.pallas_call(
        functools.partial(_attn_sample_kernel, scale=HEAD_DIM ** -0.5),
        grid=(n // nb,),
        in_specs=[pl.BlockSpec((nb, tn, 3 * A_HEADS, HEAD_DIM), lambda b: (b, 0, 0, 0)), s1, s2, s1, s2],
        out_specs=pl.BlockSpec((nb, tn, A_HEADS, HEAD_DIM), lambda b: (b, 0, 0, 0)),
        out_shape=jax.ShapeDtypeStruct((n, tn, A_HEADS, HEAD_DIM), F32),
        compiler_params=_cparams("parallel"), name="attn_sample",
    )(x4, k1, k2, v1, v2)


def _unit_lower_inverse_minus_identity(low):
    c = low.shape[-1]
    base = min(16, c)
    row = lax.broadcasted_iota(jnp.int32, (1, c, c), 1)
    col = lax.broadcasted_iota(jnp.int32, (1, c, c), 2)
    nil = jnp.where((row // base) == (col // base), -low, 0.0)
    q = nil
    pw = nil
    for _ in range(int(math.log2(base)) - 1):
        pw = _bmm(pw, pw)
        q = q + pw + _bmm(q, pw)
    b = base
    while b < c:
        sib = ((row // (2 * b)) == (col // (2 * b))) & ((row // b) != (col // b))
        off = jnp.where(sib, low, 0.0)
        t = off + _bmm(q, off)
        q = q - t - _bmm(t, q)
        b *= 2
    return q


def _dn_prompt_kernel(x_ref, ba_ref, cw_ref, adt_ref, cinit_ref, sinit_ref, o_ref, sout_ref,
                      xbuf, s_ref, *, c, nb):
    ci = pl.program_id(0)
    halo = 8
    hk = B_HEADS * B_DK
    pairs = [(b, h) for b in range(nb) for h in range(B_HEADS)]

    @pl.when(ci == 0)
    def _():
        xbuf[:, 0:halo, :] = cinit_ref[...]
        s_ref[...] = sinit_ref[...].reshape(s_ref.shape)

    xbuf[:, halo:halo + c, :] = x_ref[...]
    cw = cw_ref[...]
    y = cw[0:1, :] * xbuf[:, pl.ds(halo - B_CONV + 1, c), :]
    for i in range(1, B_CONV):
        y = y + cw[i:i + 1, :] * xbuf[:, pl.ds(halo - B_CONV + 1 + i, c), :]
    xbuf[:, 0:halo, :] = xbuf[:, c:c + halo, :]
    cq = y * _sigmoid(y)

    ba = ba_ref[...]
    beta_all = _sigmoid(ba)
    g_all = -jnp.exp(adt_ref[0:1, :]) * _softplus(ba + adt_ref[1:2, :])
    row = lax.broadcasted_iota(jnp.int32, (c, c), 0)
    col = lax.broadcasted_iota(jnp.int32, (c, c), 1)
    tri_l = (row >= col).astype(F32)
    tri_u = (row <= col).astype(F32)
    gcum_col = [_dot_f32(tri_l, g_all[b]) for b in range(nb)]
    gcum_row = [_dot_tn_f32(g_all[b], tri_u) for b in range(nb)]

    def heads(off):
        return jnp.stack([cq[b, :, off + h * B_DK:off + (h + 1) * B_DK] for b, h in pairs])

    q, k, v = heads(0), heads(hk), heads(2 * hk)
    q = q * lax.rsqrt(jnp.sum(q * q, axis=-1, keepdims=True) + EPS) * (B_DK ** -0.5)
    k = k * lax.rsqrt(jnp.sum(k * k, axis=-1, keepdims=True) + EPS)
    bl, al = BA_LANE, BA_LANE + B_HEADS
    beta = jnp.stack([beta_all[b, :, bl + h:bl + h + 1] for b, h in pairs])
    gc = jnp.stack([gcum_col[b][:, al + h:al + h + 1] for b, h in pairs])
    gr = jnp.stack([gcum_row[b][al + h:al + h + 1, :] for b, h in pairs])
    decay = jnp.exp(jnp.where((row >= col)[None], gc - gr, NEG))
    kb = k * beta
    low = jnp.where((row > col)[None], _bmm_nt(kb, k) * decay, 0.0)
    qinv = _unit_lower_inverse_minus_identity(low)
    eg = jnp.exp(gc)
    rhs = jnp.concatenate([v * beta, kb * eg], axis=2)
    sol = rhs + _bmm(qinv, rhs)
    u = sol[:, :, 0:B_DK]
    w = sol[:, :, B_DK:2 * B_DK]
    attn = _bmm_nt(q, k) * decay
    glast = gc[:, c - 1:c, :]
    kd = k * jnp.exp(glast - gc)
    s = s_ref[...]
    v_new = u - _bmm(w, s)
    o = _bmm(q * eg, s) + _bmm(attn, v_new)
    s_ref[...] = s * jnp.exp(glast) + _bmm_tn(kd, v_new)
    for b in range(nb):
        o_ref[b] = jnp.concatenate([o[b * B_HEADS + h] for h in range(B_HEADS)], axis=1)

    @pl.when(ci == pl.num_programs(0) - 1)
    def _():
        sout_ref[...] = s_ref[...].reshape(sout_ref.shape)


def _dn_prompt(proj, conv_w, adt, conv_init, s_init):
    n, t, _ = proj.shape
    c = math.gcd(t, DN_CHUNK)
    width = 3 * B_HEADS * B_DK
    return pl.pallas_call(
        functools.partial(_dn_prompt_kernel, c=c, nb=n),
        grid=(t // c,),
        in_specs=[pl.BlockSpec((n, c, width), lambda i: (0, i, 1)),
                  pl.BlockSpec((n, c, LANES), lambda i: (0, i, proj.shape[2] // LANES - 1)),
                  pl.BlockSpec((B_CONV, width), lambda i: (0, 0)),
                  pl.BlockSpec((2, LANES), lambda i: (0, 0)),
                  pl.BlockSpec((n, 8, width), lambda i: (0, 0, 0)),
                  pl.BlockSpec((n, B_HEADS, B_DK, B_DK), lambda i: (0, 0, 0, 0))],
        out_specs=[pl.BlockSpec((n, c, B_HEADS * B_DK), lambda i: (0, i, 0)),
                   pl.BlockSpec((n, B_HEADS, B_DK, B_DK), lambda i: (0, 0, 0, 0))],
        out_shape=[jax.ShapeDtypeStruct((n, t, B_HEADS * B_DK), F32),
                   jax.ShapeDtypeStruct((n, B_HEADS, B_DK, B_DK), F32)],
        scratch_shapes=[pltpu.VMEM((n, 8 + c, width), F32), pltpu.VMEM((n * B_HEADS, B_DK, B_DK), F32)],
        compiler_params=_cparams("arbitrary"), name="deltanet_prompt",
    )(proj, proj, conv_w, adt, conv_init, s_init)


DN_SAMPLE_SEQS = 8


def _dn_sample_kernel(x_ref, bat_ref, cw_ref, adt_ref, cs_ref, sinit_ref, o_ref, s_ref):
    tn = 4
    hk = B_HEADS * B_DK
    seqs = range(DN_SAMPLE_SEQS)
    xp = jnp.concatenate([cs_ref[...], x_ref[...]], axis=1)
    cw = cw_ref[...]
    y = cw[0] * xp[:, 0:tn]
    for i in range(1, B_CONV):
        y = y + cw[i] * xp[:, i:i + tn]
    cq = y * _sigmoid(y)
    q = cq[:, :, 0:B_HEADS]
    k = cq[:, :, B_HEADS:2 * B_HEADS]
    v = cq[:, :, 2 * B_HEADS:3 * B_HEADS]
    q = q * lax.rsqrt(jnp.sum(q * q, axis=-1, keepdims=True) + EPS) * (B_DK ** -0.5)
    k = k * lax.rsqrt(jnp.sum(k * k, axis=-1, keepdims=True) + EPS)
    bat = bat_ref[...]
    beta = _sigmoid(bat[:, BA_LANE:BA_LANE + B_HEADS, :])
    a = jnp.exp(-jnp.exp(adt_ref[:, 0:1]) * _softplus(bat[:, BA_LANE + B_HEADS:LANES, :] + adt_ref[:, 1:2]))
    lane_head = lax.broadcasted_iota(jnp.int32, (B_HEADS, hk), 1) // B_DK
    head_mask = lane_head == lax.broadcasted_iota(jnp.int32, (B_HEADS, hk), 0)

    def block_diag(x):
        return jnp.where(head_mask, jnp.concatenate([x] * B_HEADS, axis=1), 0.0)

    s_ref[...] = sinit_ref[...]
    for t in range(tn):
        kbd = [block_diag(k[b, t]) for b in seqs]
        ks = [_dot(kbd[b], s_ref[b]) for b in seqs]
        w = [beta[b, :, t:t + 1] * (v[b, t] - a[b, :, t:t + 1] * ks[b]) for b in seqs]
        upd = [_dot_tn(kbd[b], w[b]) for b in seqs]
        for b in seqs:
            for h in range(B_HEADS):
                rows = slice(h * B_DK, (h + 1) * B_DK)
                s_ref[b, rows, :] = s_ref[b, rows, :] * a[b, h:h + 1, t:t + 1] + upd[b][rows, :]
        for b in seqs:
            o_ref[b, t] = _dot(block_diag(q[b, t]), s_ref[b])


def _dn_sample(x4, bat, conv_w, adt_col, conv_state, s_init):
    n = x4.shape[0]
    nb = DN_SAMPLE_SEQS
    rows = 3 * B_HEADS
    hk = B_HEADS * B_DK
    return pl.pallas_call(
        _dn_sample_kernel,
        grid=(n // nb,),
        in_specs=[pl.BlockSpec((nb, 4, rows, B_DK), lambda b: (b, 0, 1, 0)),
                  pl.BlockSpec((nb, LANES, 4), lambda b: (b, 0, 0)),
                  pl.BlockSpec((B_CONV, rows, B_DK), lambda b: (0, 0, 0)),
                  pl.BlockSpec((B_HEADS, 2), lambda b: (0, 0)),
                  pl.BlockSpec((nb, B_CONV - 1, rows, B_DK), lambda b: (b, 0, 0, 0)),
                  pl.BlockSpec((nb, hk, B_DK), lambda b: (b, 0, 0))],
        out_specs=[pl.BlockSpec((nb, 4, B_HEADS, B_DK), lambda b: (b, 0, 0, 0)),
                   pl.BlockSpec((nb, hk, B_DK), lambda b: (b, 0, 0))],
        out_shape=[jax.ShapeDtypeStruct((n, 4, B_HEADS, B_DK), F32),
                   jax.ShapeDtypeStruct((n, hk, B_DK), F32)],
        compiler_params=_cparams("parallel"), name="deltanet_sample",
    )(x4, bat, conv_w, adt_col, conv_state, s_init)


def _gla_kernel(q_ref, k_ref, v_ref, glr_ref, w2_ref, gb_ref, sinit_ref, o_ref, sout_ref,
                s_ref, *, c_real, c, dk, dv, nb):
    ci = pl.program_id(1)
    sub = min(GLA_SUB, c)

    @pl.when(ci == 0)
    def _():
        s_ref[...] = sinit_ref[...]

    def load(ref, b):
        x = ref[b]
        if c_real < c:
            x = jnp.concatenate([x, jnp.zeros((c - c_real, x.shape[1]), F32)], axis=0)
        return x

    row = lax.broadcasted_iota(jnp.int32, (c, c), 0)
    col = lax.broadcasted_iota(jnp.int32, (c, c), 1)
    tri = (row >= col).astype(F32)
    row_valid = lax.broadcasted_iota(jnp.int32, (c, 1), 0) < c_real
    sub_row = lax.broadcasted_iota(jnp.int32, (sub, 1), 0)
    sub_col = lax.broadcasted_iota(jnp.int32, (sub, c), 1)
    key_row = lax.broadcasted_iota(jnp.int32, (c, 1), 0)
    ones = jnp.ones((c, LANES), F32)
    starts = range(0, c, sub)
    seqs = []
    for b in range(nb):
        pre = _dot(load(glr_ref, b), w2_ref[...]) + gb_ref[...]
        la = jnp.where(row_valid, (jnp.minimum(pre, 0.0) - jnp.log(1.0 + jnp.exp(-jnp.abs(pre)))) / C_GATE_TAU, 0.0)
        gc = _dot_f32(tri, la)
        seqs.append((la, gc, load(q_ref, b) * (dk ** -0.5), load(k_ref, b), load(v_ref, b)))
    spans = [gc[lo:lo + 1, :] - gc[lo + sub - 1:lo + sub, :] for _, gc, _, _, _ in seqs for lo in starts]
    safe = jnp.max(functools.reduce(jnp.maximum, spans)) <= GLA_SAFE_SPAN

    def head(b, h):
        la, gc, q, k, v = seqs[b]
        ks = slice(h * dk, (h + 1) * dk)
        return la[:, ks], gc[:, ks], q[:, ks], k[:, ks], v[:, h * dv:(h + 1) * dv]

    for b in range(nb):
        outs = []
        for h in range(C_HEADS):
            la, gc, q, k, v = head(b, h)
            rows = []
            for lo in starts:
                ref_pt = gc[lo:lo + 1, :]
                qsc = q[lo:lo + sub, :] * jnp.exp(gc[lo:lo + sub, :] - ref_pt)
                limit = jnp.where(safe, lo + sub, lo)
                ksc = jnp.where(key_row < limit, k * jnp.exp(jnp.minimum(ref_pt - gc, GLA_SAFE_SPAN)), 0.0)
                rows.append(_dot_nt(qsc, ksc))
            attn = jnp.where(row >= col, jnp.concatenate(rows, axis=0), 0.0)
            s = s_ref[b, h]
            o = _dot(attn, v) + _dot(q * jnp.exp(gc), s)
            glast = gc[c - 1:c, :]
            kd = k * jnp.exp(glast - gc)
            gl_col = jnp.exp(_dot_tn_f32(la, ones))[:, 0:1]
            s_ref[b, h] = s * gl_col + _dot_tn(kd, v)
            outs.append(o[0:c_real, :])
        o_ref[b] = jnp.concatenate(outs, axis=1)

    @pl.when(jnp.logical_not(safe))
    def _():
        for b in range(nb):
            fixes = []
            for h in range(C_HEADS):
                _, gc, q, k, v = head(b, h)
                rows = []
                for lo in starts:
                    qb = q[lo:lo + sub, :]
                    gblk = gc[lo:lo + sub, :]
                    blk = jnp.zeros((sub, c), F32)
                    for jj in range(sub):
                        j = lo + jj
                        e = jnp.exp(jnp.minimum(gblk - gc[j:j + 1, :], 0.0))
                        a = jnp.sum(qb * k[j:j + 1, :] * e, axis=-1, keepdims=True)
                        blk = jnp.where((sub_col == j) & (sub_row >= jj), a, blk)
                    rows.append(blk)
                fixes.append(_dot(jnp.concatenate(rows, axis=0), v)[0:c_real, :])
            o_ref[b] += jnp.concatenate(fixes, axis=1)

    @pl.when(ci == pl.num_programs(1) - 1)
    def _():
        sout_ref[...] = s_ref[...]


def _gla(proj, w2, gb, s_init, nb):
    n, t, _ = proj.shape
    dk, dv = s_init.shape[2], s_init.shape[3]
    c_real = math.gcd(t, GLA_CHUNK)
    c = max(c_real, 8)
    hk, hv = C_HEADS * dk, C_HEADS * dv
    return pl.pallas_call(
        functools.partial(_gla_kernel, c_real=c_real, c=c, dk=dk, dv=dv, nb=nb),
        grid=(n // nb, t // c_real),
        in_specs=[pl.BlockSpec((nb, c_real, hk), lambda b, i: (b, i, 0)),
                  pl.BlockSpec((nb, c_real, hk), lambda b, i: (b, i, 1)),
                  pl.BlockSpec((nb, c_real, hv), lambda b, i: (b, i, 2 * hk // hv)),
                  pl.BlockSpec((nb, c_real, LANES), lambda b, i: (b, i, (2 * hk + 2 * hv) // LANES)),
                  pl.BlockSpec((LANES, hk), lambda b, i: (0, 0)),
                  pl.BlockSpec((1, hk), lambda b, i: (0, 0)),
                  pl.BlockSpec((nb, C_HEADS, dk, dv), lambda b, i: (b, 0, 0, 0))],
        out_specs=[pl.BlockSpec((nb, c_real, hv), lambda b, i: (b, i, 0)),
                   pl.BlockSpec((nb, C_HEADS, dk, dv), lambda b, i: (b, 0, 0, 0))],
        out_shape=[jax.ShapeDtypeStruct((n, t, hv), F32),
                   jax.ShapeDtypeStruct((n, C_HEADS, dk, dv), F32)],
        scratch_shapes=[pltpu.VMEM((nb, C_HEADS, dk, dv), F32)],
        compiler_params=_cparams("parallel", "arbitrary"), name="gla",
    )(proj, proj, proj, proj, w2, gb, s_init)


def _gated_headnorm(o, z, nw, heads, width):
    parts = []
    for h in range(heads):
        zz = z[:, h * width:(h + 1) * width]
        parts.append(_rms(o[:, h * width:(h + 1) * width], nw) * (zz * _sigmoid(zz)))
    return jnp.concatenate(parts, axis=1)


def _row_parts(tm):
    tp = tm // MIX_PARTS if tm % (8 * MIX_PARTS) == 0 else tm
    return [slice(lo, lo + tp) for lo in range(0, tm, tp)]


def _mix_even_kernel(oa_ref, ob_ref, z_ref, nw_ref, w_ref, r_ref, g_ref, out_ref):
    ka = A_HEADS * HEAD_DIM
    for rows in _row_parts(out_ref.shape[0]):
        obn = _gated_headnorm(ob_ref[rows, :], z_ref[rows, :], nw_ref[...], B_HEADS, B_DK)
        m = _dot(oa_ref[rows, :], w_ref[0:ka, :]) + _dot(obn, w_ref[ka:, :])
        out_ref[rows, :] = r_ref[rows, :] + _rms(m, g_ref[...])


def _mix_even(o_a, o_b, proj, nw, w_out, r, g_post, tm):
    m, d = r.shape
    ka = o_a.shape[1]
    kb = o_b.shape[1]
    z_block = (2 * 3 * A_HEADS * HEAD_DIM) // kb
    return pl.pallas_call(
        _mix_even_kernel, grid=(m // tm,),
        in_specs=[pl.BlockSpec((tm, ka), lambda i: (i, 0)),
                  pl.BlockSpec((tm, kb), lambda i: (i, 0)),
                  pl.BlockSpec((tm, kb), lambda i: (i, z_block)),
                  pl.BlockSpec((1, B_DK), lambda i: (0, 0)),
                  pl.BlockSpec((ka + kb, d), lambda i: (0, 0)),
                  pl.BlockSpec((tm, d), lambda i: (i, 0)),
                  pl.BlockSpec((1, d), lambda i: (0, 0))],
        out_specs=pl.BlockSpec((tm, d), lambda i: (i, 0)),
        out_shape=jax.ShapeDtypeStruct((m, d), F32),
        compiler_params=_cparams("parallel"), name="mix_even",
    )(o_a, o_b, proj, nw.reshape(1, -1), w_out, r, g_post.reshape(1, d))


def _mix_odd_kernel(o_ref, z_ref, nw_ref, w_ref, r_ref, g_ref, out_ref, *, dv):
    for rows in _row_parts(out_ref.shape[0]):
        on = _gated_headnorm(o_ref[rows, :], z_ref[rows, :], nw_ref[...], C_HEADS, dv)
        out_ref[rows, :] = r_ref[rows, :] + _rms(_dot(on, w_ref[...]), g_ref[...])


def _mix_odd(o, proj, nw, w_out, r, g_post, tm):
    m, d = r.shape
    kv = o.shape[1]
    return pl.pallas_call(
        functools.partial(_mix_odd_kernel, dv=kv // C_HEADS), grid=(m // tm,),
        in_specs=[pl.BlockSpec((tm, kv), lambda i: (i, 0)),
                  pl.BlockSpec((tm, kv), lambda i: (i, 2)),
                  pl.BlockSpec((1, kv // C_HEADS), lambda i: (0, 0)),
                  pl.BlockSpec((kv, d), lambda i: (0, 0)),
                  pl.BlockSpec((tm, d), lambda i: (i, 0)),
                  pl.BlockSpec((1, d), lambda i: (0, 0))],
        out_specs=pl.BlockSpec((tm, d), lambda i: (i, 0)),
        out_shape=jax.ShapeDtypeStruct((m, d), F32),
        compiler_params=_cparams("parallel"), name="mix_odd",
    )(o, proj, nw.reshape(1, -1), w_out, r, g_post.reshape(1, d))


def _ffn_kernel(x_ref, gpre_ref, wg_ref, wv_ref, cwg_ref, cwv_ref, wd_ref, gpost_ref, ig_ref, iv_ref,
                o_ref, tg_ref, tv_ref, h_ref, ug, uv, *carry, tm, halo, shift):
    i = pl.program_id(1)
    j = pl.program_id(2)

    @pl.when(j == 0)
    def _():
        h_ref[...] = _rms(x_ref[0], gpre_ref[...]).astype(BF16)
        o_ref[0] = jnp.zeros_like(o_ref[0])

    if carry:
        cg, cv = carry

        @pl.when(i == 0)
        def _():
            ug[0:halo, :] = ig_ref[0]
            uv[0:halo, :] = iv_ref[0]

        @pl.when(i > 0)
        def _():
            ug[0:halo, :] = cg[j]
            uv[0:halo, :] = cv[j]
    else:
        ug[0:halo, :] = ig_ref[0]
        uv[0:halo, :] = iv_ref[0]

    tf = ug.shape[1]
    sub = min(FFN_SUB, tf)
    cols = [slice(s * sub, (s + 1) * sub) for s in range(tf // sub)]
    for cs in cols:
        ug[halo:halo + tm, cs] = jnp.dot(h_ref[...], wg_ref[:, cs], preferred_element_type=F32)
        uv[halo:halo + tm, cs] = jnp.dot(h_ref[...], wv_ref[:, cs], preferred_element_type=F32)

    def conv(u, cw_ref, cs):
        y = cw_ref[0:1, cs] * u[pl.ds(halo - (FFN_CONV - 1) * shift, tm), cs]
        for tap in range(1, FFN_CONV):
            y = y + cw_ref[tap:tap + 1, cs] * u[pl.ds(halo - (FFN_CONV - 1 - tap) * shift, tm), cs]
        return y

    acc = None
    for cs in cols:
        act = _gelu_tanh(conv(ug, cwg_ref, cs)) * conv(uv, cwv_ref, cs)
        part = _dot(act, wd_ref[cs, :])
        acc = part if acc is None else acc + part
    tail_g = ug[tm:tm + halo, :]
    tail_v = uv[tm:tm + halo, :]
    tg_ref[0, 0] = tail_g
    tv_ref[0, 0] = tail_v
    if carry:
        cg[j] = tail_g
        cv[j] = tail_v
    o_ref[0] += acc

    @pl.when(j == pl.num_programs(2) - 1)
    def _():
        o_ref[0] = x_ref[0] + _rms(o_ref[0], gpost_ref[...])


def _ffn(x, g_pre, w_up, conv_w, w_down, g_post, init, layer, tm, tf, shift):
    n, t, d = x.shape
    dff = w_down.shape[1]
    halo = init.shape[1]
    ni, nj = t // tm, dff // tf
    scratch = [pltpu.VMEM((tm, d), BF16), pltpu.VMEM((halo + tm, tf), F32), pltpu.VMEM((halo + tm, tf), F32)]
    if ni > 1:
        scratch += [pltpu.VMEM((nj, halo, tf), F32), pltpu.VMEM((nj, halo, tf), F32)]
    out, tail_g, tail_v = pl.pallas_call(
        functools.partial(_ffn_kernel, tm=tm, halo=halo, shift=shift),
        grid=(n, ni, nj),
        in_specs=[pl.BlockSpec((1, tm, d), lambda b, i, j: (b, i, 0)),
                  pl.BlockSpec((1, d), lambda b, i, j: (0, 0)),
                  pl.BlockSpec((None, d, tf), lambda b, i, j: (layer, 0, j)),
                  pl.BlockSpec((None, d, tf), lambda b, i, j: (layer, 0, j + nj)),
                  pl.BlockSpec((None, FFN_CONV, tf), lambda b, i, j: (layer, 0, j)),
                  pl.BlockSpec((None, FFN_CONV, tf), lambda b, i, j: (layer, 0, j + nj)),
                  pl.BlockSpec((None, tf, d), lambda b, i, j: (layer, j, 0)),
                  pl.BlockSpec((1, d), lambda b, i, j: (0, 0)),
                  pl.BlockSpec((1, halo, tf), lambda b, i, j: (b, 0, j)),
                  pl.BlockSpec((1, halo, tf), lambda b, i, j: (b, 0, j + nj))],
        out_specs=[pl.BlockSpec((1, tm, d), lambda b, i, j: (b, i, 0), pipeline_mode=pl.Buffered(1)),
                   pl.BlockSpec((1, 1, halo, tf), lambda b, i, j: (b, i, 0, j)),
                   pl.BlockSpec((1, 1, halo, tf), lambda b, i, j: (b, i, 0, j))],
        out_shape=[jax.ShapeDtypeStruct((n, t, d), F32),
                   jax.ShapeDtypeStruct((n, ni, halo, dff), F32),
                   jax.ShapeDtypeStruct((n, ni, halo, dff), F32)],
        scratch_shapes=scratch,
        compiler_params=_cparams("parallel", "arbitrary", "arbitrary"), name="conv_ffn",
    )(x, g_pre.reshape(1, d), w_up, w_up, conv_w, conv_w, w_down, g_post.reshape(1, d), init, init)
    return out, jnp.concatenate([tail_g[:, ni - 1], tail_v[:, ni - 1]], axis=-1)


def _ple_kernel(r_ref, p_ref, wp_ref, wg_ref, o_ref):
    r = r_ref[...]
    o_ref[...] = r + _dot(p_ref[...], wp_ref[...]) * _sigmoid(_dot(r, wg_ref[...]))


def _ple(r, p, w_proj, w_gate, layer, tm):
    m, d = r.shape
    pd = p.shape[1]
    return pl.pallas_call(
        _ple_kernel, grid=(m // tm,),
        in_specs=[pl.BlockSpec((tm, d), lambda i: (i, 0)),
                  pl.BlockSpec((tm, pd), lambda i: (i, 0)),
                  pl.BlockSpec((None, pd, d), lambda i: (layer, 0, 0)),
                  pl.BlockSpec((None, d, d), lambda i: (layer, 0, 0))],
        out_specs=pl.BlockSpec((tm, d), lambda i: (i, 0)),
        out_shape=jax.ShapeDtypeStruct((m, d), F32),
        compiler_params=_cparams("parallel"), name="ple",
    )(r, p, w_proj, w_gate)


def _reorder_cols_kernel(wt_ref, o_ref, *, spans):
    col = 0
    for span in spans:
        if isinstance(span, int):
            o_ref[:, col:col + span] = jnp.zeros((o_ref.shape[0], span), BF16)
            col += span
        else:
            a, b = span
            rows = -(-(b - a) // LANES) * LANES
            piece = wt_ref[a:a + rows, :].T
            o_ref[:, col:col + b - a] = piece[:, 0:b - a].astype(BF16)
            col += b - a


def _reorder_cols(w, spans):
    _, k, n_in = w.shape
    n_out = sum(s if isinstance(s, int) else s[1] - s[0] for s in spans)
    assert all(isinstance(s, int) or s[0] % 8 == 0 and s[0] + -(-(s[1] - s[0]) // LANES) * LANES <= n_in for s in spans)
    tr = WEIGHT_PREP_ROWS
    return pl.pallas_call(
        functools.partial(_reorder_cols_kernel, spans=spans),
        grid=(k // tr,),
        in_specs=[pl.BlockSpec((None, n_in, tr), lambda i: (0, 0, i))],
        out_specs=pl.BlockSpec((tr, n_out), lambda i: (i, 0)),
        out_shape=jax.ShapeDtypeStruct((k, n_out), BF16),
        compiler_params=_cparams("parallel"), name="weight_prep",
    )(jnp.swapaxes(w, 1, 2))


def _prep_weights(w_in_even, w_out_even, w_in_odd, w_out_odd, gla_gate_w2, ffn_w_up, ffn_w_down,
                  ple_w_proj, ple_w_gate, dn_a_log, dn_dt_bias):
    ab = 3 * A_HEADS * HEAD_DIM + 3 * B_HEADS * B_DK
    nl = 2 * B_HEADS
    n0 = w_in_even.shape[2]
    gap = -n0 % PROJ0_TN
    assert gap >= LANES - nl
    w0 = _reorder_cols(w_in_even, [(0, ab), (ab + nl, n0), gap, (ab, ab + nl)])
    d = w_in_odd.shape[1]
    qkv = 2 * (d // 2) + d
    n1 = w_in_odd.shape[2]
    w1 = _reorder_cols(w_in_odd, [(0, qkv), (qkv + C_GATE_RANK, n1), (qkv, qkv + C_GATE_RANK), -n1 % PROJ_TN])
    w2 = jnp.pad(gla_gate_w2[0], ((0, LANES - C_GATE_RANK), (0, 0))).astype(BF16)
    adt_row = jnp.pad(jnp.stack([dn_a_log[0], dn_dt_bias[0]]), ((0, 0), (BA_LANE + B_HEADS, 0)))
    adt_col = jnp.stack([dn_a_log[0], dn_dt_bias[0]], axis=1)
    return dict(w0=w0, w1=w1, w2=w2, adt_row=adt_row, adt_col=adt_col,
                w_out_even=w_out_even[0].astype(BF16), w_out_odd=w_out_odd[0].astype(BF16),
                w_up=ffn_w_up.astype(BF16), w_down=ffn_w_down.astype(BF16),
                w_proj=ple_w_proj.astype(BF16), w_gate=ple_w_gate.astype(BF16))


def _layer_tail(r, p, layer, wt, prm, ffn_init, n, t, tm):
    d = r.shape[1]
    r, tail = _ffn(r.reshape(n, t, d), prm["norm_ffn_pre"][layer], wt["w_up"], prm["ffn_conv_w"],
                   wt["w_down"], prm["norm_ffn_post"][layer], ffn_init, layer, PROMPT_FFN_TM, PROMPT_FFN_TF, 1)
    r = _ple(r.reshape(n * t, d), p, wt["w_proj"], wt["w_gate"], layer, tm)
    return r, tail


def _prompt_group(x, p, wt, prm):
    n, t, d = x.shape
    m = n * t
    tm = MIX_TM
    dff2 = prm["ffn_conv_w"].shape[-1]
    r = x.reshape(m, d)
    proj = _rms_matmul(r, prm["norm_mix_pre"][0], wt["w0"], PROMPT_PROJ_TM, PROJ0_TN)
    pv = proj.reshape(n, t, -1)
    o_a = _attn_prompt(pv, n, t).reshape(m, -1)
    width = 3 * B_HEADS * B_DK
    o_b, dn_state = _dn_prompt(pv, prm["dn_conv_w"][0], wt["adt_row"], jnp.zeros((n, 8, width), F32),
                               jnp.zeros((n, B_HEADS, B_DK, B_DK), F32))
    r = _mix_even(o_a, o_b.reshape(m, -1), proj, prm["dn_norm_w"][0], wt["w_out_even"], r,
                  prm["norm_mix_post"][0], tm)
    keep = min(A_WMAX, t)
    hd = A_HEADS * HEAD_DIM
    win_k = pv[:, t - keep:, hd:2 * hd].reshape(n, keep, A_HEADS, HEAD_DIM)
    win_v = pv[:, t - keep:, 2 * hd:3 * hd].reshape(n, keep, A_HEADS, HEAD_DIM)
    dn_conv = pv[:, t - (B_CONV - 1):, width:2 * width]
    ffn_zero = jnp.zeros((n, 8, dff2), F32)
    r, tail0 = _layer_tail(r, p[0].reshape(m, -1), 0, wt, prm, ffn_zero, n, t, tm)
    proj1 = _rms_matmul(r, prm["norm_mix_pre"][1], wt["w1"], PROMPT_PROJ_TM, PROJ_TN)
    o_c, gla_state = _gla(proj1.reshape(n, t, -1), wt["w2"], prm["gla_gate_b"][0].reshape(1, -1),
                          jnp.zeros((n, C_HEADS, d // (2 * C_HEADS), d // C_HEADS), F32), n)
    r = _mix_odd(o_c.reshape(m, -1), proj1, prm["gla_norm_w"][0], wt["w_out_odd"], r, prm["norm_mix_post"][1], tm)
    r, tail1 = _layer_tail(r, p[1].reshape(m, -1), 1, wt, prm, ffn_zero, n, t, tm)
    ffn_conv = jnp.stack([tail0[:, 8 - (FFN_CONV - 1):], tail1[:, 8 - (FFN_CONV - 1):]])
    return (r.reshape(n, t, d), win_k[None], win_v[None], dn_conv[None], dn_state[None], gla_state[None], ffn_conv)


def _sample_group(x, p, cache_k, cache_v, dn_conv_state, dn_state, gla_state, ffn_state, wt, prm):
    n, t, d = x.shape
    m = n * t
    tm = MIX_TM
    r = x.reshape(m, d)

    def time_major(a):
        return a.reshape(n, t, -1).transpose(1, 0, 2).reshape(1, m, -1)

    def seq_major(a):
        return a.reshape(t, n, -1).transpose(1, 0, 2).reshape(m, -1)

    def ffn_layer(r, layer):
        init = ffn_state[layer].transpose(1, 0, 2).reshape(1, (FFN_CONV - 1) * n, -1)
        rt, tail = _ffn(time_major(r), prm["norm_ffn_pre"][layer], wt["w_up"], prm["ffn_conv_w"],
                        wt["w_down"], prm["norm_ffn_post"][layer], init, layer, m, SAMPLE_FFN_TF, n)
        r = _ple(seq_major(rt), p[layer].reshape(m, -1), wt["w_proj"], wt["w_gate"], layer, tm)
        return r, tail.reshape(FFN_CONV - 1, n, -1).transpose(1, 0, 2)

    proj = _rms_matmul(r, prm["norm_mix_pre"][0], wt["w0"], m, PROJ0_TN)
    width = 3 * B_HEADS * B_DK
    heads = proj.reshape(n, t, -1, HEAD_DIM)
    o_a = _attn_sample(heads, cache_k[0], cache_v[0])
    bat = heads[:, :, heads.shape[2] - 1].transpose(0, 2, 1)
    o_b, dn_new = _dn_sample(heads, bat, prm["dn_conv_w"][0].reshape(B_CONV, -1, B_DK), wt["adt_col"],
                             dn_conv_state[0].reshape(n, B_CONV - 1, -1, B_DK),
                             dn_state[0].reshape(n, B_HEADS * B_DK, B_DK))
    r = _mix_even(o_a.reshape(m, -1), o_b.reshape(m, -1), proj, prm["dn_norm_w"][0], wt["w_out_even"], r,
                  prm["norm_mix_post"][0], tm)
    win_k = heads[:, :, A_HEADS:2 * A_HEADS]
    win_v = heads[:, :, 2 * A_HEADS:3 * A_HEADS]
    dn_conv = heads[:, t - (B_CONV - 1):, 3 * A_HEADS:3 * (A_HEADS + B_HEADS)].reshape(n, B_CONV - 1, width)
    r, tail0 = ffn_layer(r, 0)
    proj1 = _rms_matmul(r, prm["norm_mix_pre"][1], wt["w1"], m, PROJ_TN)
    o_c, gla_new = _gla(proj1.reshape(n, t, -1), wt["w2"], prm["gla_gate_b"][0].reshape(1, -1), gla_state[0],
                        SAMPLE_GLA_SEQS)
    r = _mix_odd(o_c.reshape(m, -1), proj1, prm["gla_norm_w"][0], wt["w_out_odd"], r, prm["norm_mix_post"][1], tm)
    r, tail1 = ffn_layer(r, 1)
    return (r.reshape(n, t, d), win_k[None], win_v[None], dn_conv[None],
            dn_new.reshape(dn_state.shape), gla_new[None], jnp.stack([tail0, tail1]))


def kernel(x_prompt, x_sample, cache_win_k, cache_win_v, state_dn_conv, state_dn, state_gla, state_ffn_conv, p_prompt, p_sample, norm_mix_pre, norm_mix_post, norm_ffn_pre, norm_ffn_post, w_in_even, w_out_even, dn_conv_w, dn_a_log, dn_dt_bias, dn_norm_w, w_in_odd, gla_gate_w2, gla_gate_b, gla_norm_w, w_out_odd, ffn_w_up, ffn_conv_w, ffn_w_down, ple_w_proj, ple_w_gate):
    wt = _prep_weights(w_in_even, w_out_even, w_in_odd, w_out_odd, gla_gate_w2, ffn_w_up, ffn_w_down,
                       ple_w_proj, ple_w_gate, dn_a_log, dn_dt_bias)
    prm = dict(norm_mix_pre=norm_mix_pre, norm_mix_post=norm_mix_post, norm_ffn_pre=norm_ffn_pre,
               norm_ffn_post=norm_ffn_post, dn_conv_w=dn_conv_w, dn_norm_w=dn_norm_w, gla_gate_b=gla_gate_b,
               gla_norm_w=gla_norm_w, ffn_conv_w=ffn_conv_w)
    yp, kp, vp, dcp, dsp, gsp, fcp = _prompt_group(x_prompt, p_prompt, wt, prm)
    ys, ks, vs, dcs, dss, gss, fcs = _sample_group(x_sample, p_sample, cache_win_k, cache_win_v, state_dn_conv,
                                                   state_dn, state_gla, state_ffn_conv, wt, prm)
    return (yp, ys, kp, vp, dcp, dsp, gsp, fcp, ks, vs, dcs, dss, gss, fcs)
```

```python
import functools
import math

import jax
import jax.numpy as jnp
from jax import lax
from jax.experimental import pallas as pl
from jax.experimental.pallas import tpu as pltpu

F32 = jnp.float32
BF16 = jnp.bfloat16
EPS = 1e-6
NEG = -1e30
HIGHEST = lax.Precision.HIGHEST

LANES = 128
V7X_VMEM_LIMIT_BYTES = 60 * 2**20

HEAD_DIM = 128
A_HEADS = 8
A_KEYS = 128
A_DILATIONS = (1, 4, 16)
A_WMAX = 2048
B_HEADS = 8
B_DK = 128
B_CONV = 4
C_HEADS = 4
C_GATE_RANK = 16
C_GATE_TAU = 16.0
FFN_CONV = 3
DN_CHUNK = 64
GLA_CHUNK = 64
GLA_SUB = 32
GLA_SAFE_SPAN = 60.0
PROJ_TN = 1280
PROJ0_TN = 1536
BA_LANE = 128 - 2 * 8
PROMPT_PROJ_TM = 1024
PROMPT_FFN_TM = 1024
PROMPT_FFN_TF = 512
FFN_SUB = 512
SAMPLE_FFN_TF = 512
WEIGHT_PREP_ROWS = 256
MIX_TM = 512
MIX_PARTS = 2
SAMPLE_GLA_SEQS = 4
SAMPLE_ATTN_SEQS = 2


def _cparams(*sem):
    return pltpu.CompilerParams(dimension_semantics=sem, vmem_limit_bytes=V7X_VMEM_LIMIT_BYTES)


def _dot(a, b):
    return jnp.dot(a.astype(BF16), b.astype(BF16), preferred_element_type=F32)


def _dot_nt(a, b):
    return lax.dot_general(a.astype(BF16), b.astype(BF16), (((1,), (1,)), ((), ())),
                           preferred_element_type=F32)


def _dot_tn(a, b):
    return lax.dot_general(a.astype(BF16), b.astype(BF16), (((0,), (0,)), ((), ())),
                           preferred_element_type=F32)


def _bmm(a, b):
    return lax.dot_general(a.astype(BF16), b.astype(BF16), (((2,), (1,)), ((0,), (0,))),
                           preferred_element_type=F32)


def _bmm_nt(a, b):
    return lax.dot_general(a.astype(BF16), b.astype(BF16), (((2,), (2,)), ((0,), (0,))),
                           preferred_element_type=F32)


def _bmm_tn(a, b):
    return lax.dot_general(a.astype(BF16), b.astype(BF16), (((1,), (1,)), ((0,), (0,))),
                           preferred_element_type=F32)


def _dot_f32(a, b):
    return jnp.dot(a, b, precision=HIGHEST, preferred_element_type=F32)


def _dot_tn_f32(a, b):
    return lax.dot_general(a, b, (((0,), (0,)), ((), ())), precision=HIGHEST,
                           preferred_element_type=F32)


def _rms(x, g):
    return x * lax.rsqrt(jnp.mean(x * x, axis=-1, keepdims=True) + EPS) * g


def _sigmoid(x):
    return 1.0 / (1.0 + jnp.exp(-x))


def _softplus(x):
    return jnp.maximum(x, 0.0) + jnp.log(1.0 + jnp.exp(-jnp.abs(x)))


def _gelu_tanh(x):
    return 0.5 * x * (1.0 + jnp.tanh(math.sqrt(2.0 / math.pi) * (x + 0.044715 * (x * x * x))))


def _rms_matmul_kernel(x_ref, g_ref, w_ref, o_ref, h_ref):
    @pl.when(pl.program_id(1) == 0)
    def _():
        h_ref[...] = _rms(x_ref[...], g_ref[...]).astype(BF16)

    o_ref[...] = jnp.dot(h_ref[...], w_ref[...], preferred_element_type=F32)


def _rms_matmul(x, g, w, tm, tn):
    m, k = x.shape
    n = w.shape[1]
    return pl.pallas_call(
        _rms_matmul_kernel,
        grid=(m // tm, n // tn),
        in_specs=[pl.BlockSpec((tm, k), lambda i, j: (i, 0)),
                  pl.BlockSpec((1, k), lambda i, j: (0, 0)),
                  pl.BlockSpec((k, tn), lambda i, j: (0, j))],
        out_specs=pl.BlockSpec((tm, tn), lambda i, j: (i, j)),
        out_shape=jax.ShapeDtypeStruct((m, n), F32),
        scratch_shapes=[pltpu.VMEM((tm, k), BF16)],
        compiler_params=_cparams("parallel", "arbitrary"),
        name="rms_matmul",
    )(x, g.reshape(1, k), w)


ATTN_SB = A_WMAX


def _attn_prompt_kernel(q_ref, kc_ref, kp_ref, vc_ref, vp_ref, o_ref, o_s, l_s, *, scale):
    has_prev = pl.program_id(2) > 0
    row = lax.broadcasted_iota(jnp.int32, (A_KEYS, 2 * A_KEYS), 0)
    col = lax.broadcasted_iota(jnp.int32, (A_KEYS, 2 * A_KEYS), 1)
    cur_ok = (col >= A_KEYS) & (col - A_KEYS <= row)
    prev_ok = (col < A_KEYS) & (col >= row)

    def tile(ref, start, d):
        if d == 1:
            return ref[0, pl.ds(start, A_KEYS), :]
        return ref[0, pl.ds(start, A_KEYS, stride=d), :]

    for g, d in enumerate(A_DILATIONS):
        span = A_KEYS * d
        for r in range(d):
            for j in range(ATTN_SB // span):
                start = r + span * j
                q = tile(q_ref, start, d) * scale
                kc, vc = tile(kc_ref, start, d).astype(BF16), tile(vc_ref, start, d).astype(BF16)
                if j > 0:
                    ok = cur_ok | prev_ok
                else:
                    kp = tile(kp_ref, ATTN_SB - span + r, d).astype(BF16)
                    vp = tile(vp_ref, ATTN_SB - span + r, d).astype(BF16)
                    ok = cur_ok | (prev_ok & has_prev)
                s = _dot_nt(q, jnp.concatenate([kp, kc], axis=0))
                s = jnp.where(ok, s, NEG)
                m = jnp.max(s, axis=-1, keepdims=True)
                p = jnp.exp(s - m)
                l = jnp.sum(p, axis=-1, keepdims=True)
                o = _dot(p, jnp.concatenate([vp, vc], axis=0)) / l
                kp, vp = kc, vc
                lse = jnp.broadcast_to(m + jnp.log(l), (A_KEYS, HEAD_DIM))
                if d == 1:
                    o_s[g, pl.ds(start, A_KEYS), :] = o
                    l_s[g, pl.ds(start, A_KEYS), :] = lse
                else:
                    o_s[g, pl.ds(start, A_KEYS, stride=d), :] = o
                    l_s[g, pl.ds(start, A_KEYS, stride=d), :] = lse
    la, lb, lc = l_s[0], l_s[1], l_s[2]
    m = jnp.maximum(jnp.maximum(la, lb), lc)
    wa, wb, wc = jnp.exp(la - m), jnp.exp(lb - m), jnp.exp(lc - m)
    o_ref[0] = (wa * o_s[0] + wb * o_s[1] + wc * o_s[2]) / (wa + wb + wc)


def _attn_prompt(proj, n, t):
    assert t % ATTN_SB == 0 and len(A_DILATIONS) == 3
    cur = lambda off: (lambda b, h, i: (b, i, off + h))
    prev = lambda off: (lambda b, h, i: (b, jnp.maximum(i - 1, 0), off + h))
    blk = (1, ATTN_SB, HEAD_DIM)
    return pl.pallas_call(
        functools.partial(_attn_prompt_kernel, scale=HEAD_DIM ** -0.5),
        grid=(n, A_HEADS, t // ATTN_SB),
        in_specs=[pl.BlockSpec(blk, cur(0)),
                  pl.BlockSpec(blk, cur(A_HEADS)), pl.BlockSpec(blk, prev(A_HEADS)),
                  pl.BlockSpec(blk, cur(2 * A_HEADS)), pl.BlockSpec(blk, prev(2 * A_HEADS))],
        out_specs=pl.BlockSpec(blk, lambda b, h, i: (b, i, h)),
        out_shape=jax.ShapeDtypeStruct((n, t, A_HEADS * HEAD_DIM), F32),
        scratch_shapes=[pltpu.VMEM((3, ATTN_SB, HEAD_DIM), F32), pltpu.VMEM((3, ATTN_SB, HEAD_DIM), F32)],
        compiler_params=_cparams("parallel", "parallel", "arbitrary"), name="attn_prompt",
    )(proj, proj, proj, proj, proj)


def _attn_sample_kernel(x_ref, k1_ref, k2_ref, v1_ref, v2_ref, o_ref, *, scale):
    tn = 4
    nq = tn * A_HEADS
    nk = A_KEYS * A_HEADS

    def grid_masks(cols):
        qrow = lax.broadcasted_iota(jnp.int32, (nq, cols), 0)
        kcol = lax.broadcasted_iota(jnp.int32, (nq, cols), 1)
        return (qrow % A_HEADS) == (kcol % A_HEADS), qrow // A_HEADS, kcol // A_HEADS

    same_head, qt, key = grid_masks(nk)
    mult_dense = jnp.where(same_head & (key >= qt), 1.0, 0.0)
    mult_res = [jnp.where(same_head & (qt == t), 1.0, 0.0) for t in range(tn)]
    same_head, qt, kt = grid_masks(nq)
    mult_new = jnp.where(same_head & (kt == qt), float(len(A_DILATIONS)), jnp.where(same_head & (kt < qt), 1.0, 0.0))
    flat = lambda a: a.reshape(nk, HEAD_DIM)
    dense = slice(A_KEYS - A_KEYS // 4, A_KEYS)

    for b in range(x_ref.shape[0]):
        x = x_ref[b]
        q = (x[:, 0:A_HEADS] * scale).reshape(nq, HEAD_DIM)
        knew = x[:, A_HEADS:2 * A_HEADS].reshape(nq, HEAD_DIM)
        vnew = x[:, 2 * A_HEADS:3 * A_HEADS].reshape(nq, HEAD_DIM)
        segs = [(flat(k1_ref[b, dense]), flat(v1_ref[b, dense]), mult_dense)]
        for kr, vr in ((k1_ref, v1_ref), (k2_ref, v2_ref)):
            for t in range(tn):
                segs.append((flat(kr[b, :, t]), flat(vr[b, :, t]), mult_res[t]))
        segs.append((knew, vnew, mult_new))

        scores = [_dot_nt(q, k) for k, _, _ in segs]
        m = None
        for s, (_, _, mult) in zip(scores, segs):
            ms = jnp.max(jnp.where(mult > 0.0, s, NEG), axis=-1, keepdims=True)
            m = ms if m is None else jnp.maximum(m, ms)
        probs = [jnp.where(mult > 0.0, jnp.exp(s - m), 0.0) * mult for s, (_, _, mult) in zip(scores, segs)]
        l = sum(jnp.sum(p, axis=-1, keepdims=True) for p in probs)
        acc = sum(_dot(p, v) for p, (_, v, _) in zip(probs, segs))
        o_ref[b] = (acc / l).reshape(tn, A_HEADS, HEAD_DIM)


def _attn_sample(x4, cache_k, cache_v):
    n, tn = x4.shape[0], x4.shape[1]
    w = cache_k.shape[1]
    assert w == A_WMAX and tn == 4
    views = []
    for cache in (cache_k, cache_v):
        views += [cache.reshape(n, w // 4, 4, A_HEADS, HEAD_DIM), cache.reshape(n, w // 16, 16, A_HEADS, HEAD_DIM)]
    k1, k2, v1, v2 = views
    nb = SAMPLE_ATTN_SEQS
    s1 = pl.BlockSpec((nb, A_KEYS, 4, A_HEADS, HEAD_DIM), lambda b: (b, w // 4 // A_KEYS - 1, 0, 0, 0))
    s2 = pl.BlockSpec((nb, A_KEYS, 4, A_HEADS, HEAD_DIM), lambda b: (b, 0, 0, 0, 0))
    return pl.pallas_call(
        functools.partial(_attn_sample_kernel, scale=HEAD_DIM ** -0.5),
        grid=(n // nb,),
        in_specs=[pl.BlockSpec((nb, tn, 3 * A_HEADS, HEAD_DIM), lambda b: (b, 0, 0, 0)), s1, s2, s1, s2],
        out_specs=pl.BlockSpec((nb, tn, A_HEADS, HEAD_DIM), lambda b: (b, 0, 0, 0)),
        out_shape=jax.ShapeDtypeStruct((n, tn, A_HEADS, HEAD_DIM), F32),
        compiler_params=_cparams("parallel"), name="attn_sample",
    )(x4, k1, k2, v1, v2)


def _unit_lower_inverse_minus_identity(low):
    c = low.shape[-1]
    base = min(16, c)
    row = lax.broadcasted_iota(jnp.int32, (1, c, c), 1)
    col = lax.broadcasted_iota(jnp.int32, (1, c, c), 2)
    nil = jnp.where((row // base) == (col // base), -low, 0.0)
    q = nil
    pw = nil
    for _ in range(int(math.log2(base)) - 1):
        pw = _bmm(pw, pw)
        q = q + pw + _bmm(q, pw)
    b = base
    while b < c:
        sib = ((row // (2 * b)) == (col // (2 * b))) & ((row // b) != (col // b))
        off = jnp.where(sib, low, 0.0)
        t = off + _bmm(q, off)
        q = q - t - _bmm(t, q)
        b *= 2
    return q


def _dn_prompt_kernel(x_ref, ba_ref, cw_ref, adt_ref, cinit_ref, sinit_ref, o_ref, sout_ref,
                      xbuf, s_ref, *, c, nb):
    ci = pl.program_id(0)
    halo = 8
    hk = B_HEADS * B_DK
    pairs = [(b, h) for b in range(nb) for h in range(B_HEADS)]

    @pl.when(ci == 0)
    def _():
        xbuf[:, 0:halo, :] = cinit_ref[...]
        s_ref[...] = sinit_ref[...].reshape(s_ref.shape)

    xbuf[:, halo:halo + c, :] = x_ref[...]
    cw = cw_ref[...]
    y = cw[0:1, :] * xbuf[:, pl.ds(halo - B_CONV + 1, c), :]
    for i in range(1, B_CONV):
        y = y + cw[i:i + 1, :] * xbuf[:, pl.ds(halo - B_CONV + 1 + i, c), :]
    xbuf[:, 0:halo, :] = xbuf[:, c:c + halo, :]
    cq = y * _sigmoid(y)

    ba = ba_ref[...]
    beta_all = _sigmoid(ba)
    g_all = -jnp.exp(adt_ref[0:1, :]) * _softplus(ba + adt_ref[1:2, :])
    row = lax.broadcasted_iota(jnp.int32, (c, c), 0)
    col = lax.broadcasted_iota(jnp.int32, (c, c), 1)
    tri_l = (row >= col).astype(F32)
    tri_u = (row <= col).astype(F32)
    gcum_col = [_dot_f32(tri_l, g_all[b]) for b in range(nb)]
    gcum_row = [_dot_tn_f32(g_all[b], tri_u) for b in range(nb)]

    def heads(off):
        return jnp.stack([cq[b, :, off + h * B_DK:off + (h + 1) * B_DK] for b, h in pairs])

    q, k, v = heads(0), heads(hk), heads(2 * hk)
    q = q * lax.rsqrt(jnp.sum(q * q, axis=-1, keepdims=True) + EPS) * (B_DK ** -0.5)
    k = k * lax.rsqrt(jnp.sum(k * k, axis=-1, keepdims=True) + EPS)
    bl, al = BA_LANE, BA_LANE + B_HEADS
    beta = jnp.stack([beta_all[b, :, bl + h:bl + h + 1] for b, h in pairs])
    gc = jnp.stack([gcum_col[b][:, al + h:al + h + 1] for b, h in pairs])
    gr = jnp.stack([gcum_row[b][al + h:al + h + 1, :] for b, h in pairs])
    decay = jnp.exp(jnp.where((row >= col)[None], gc - gr, NEG))
    kb = k * beta
    low = jnp.where((row > col)[None], _bmm_nt(kb, k) * decay, 0.0)
    qinv = _unit_lower_inverse_minus_identity(low)
    eg = jnp.exp(gc)
    rhs = jnp.concatenate([v * beta, kb * eg], axis=2)
    sol = rhs + _bmm(qinv, rhs)
    u = sol[:, :, 0:B_DK]
    w = sol[:, :, B_DK:2 * B_DK]
    attn = _bmm_nt(q, k) * decay
    glast = gc[:, c - 1:c, :]
    kd = k * jnp.exp(glast - gc)
    s = s_ref[...]
    v_new = u - _bmm(w, s)
    o = _bmm(q * eg, s) + _bmm(attn, v_new)
    s_ref[...] = s * jnp.exp(glast) + _bmm_tn(kd, v_new)
    for b in range(nb):
        o_ref[b] = jnp.concatenate([o[b * B_HEADS + h] for h in range(B_HEADS)], axis=1)

    @pl.when(ci == pl.num_programs(0) - 1)
    def _():
        sout_ref[...] = s_ref[...].reshape(sout_ref.shape)


def _dn_prompt(proj, conv_w, adt, conv_init, s_init):
    n, t, _ = proj.shape
    c = math.gcd(t, DN_CHUNK)
    width = 3 * B_HEADS * B_DK
    return pl.pallas_call(
        functools.partial(_dn_prompt_kernel, c=c, nb=n),
        grid=(t // c,),
        in_specs=[pl.BlockSpec((n, c, width), lambda i: (0, i, 1)),
                  pl.BlockSpec((n, c, LANES), lambda i: (0, i, proj.shape[2] // LANES - 1)),
                  pl.BlockSpec((B_CONV, width), lambda i: (0, 0)),
                  pl.BlockSpec((2, LANES), lambda i: (0, 0)),
                  pl.BlockSpec((n, 8, width), lambda i: (0, 0, 0)),
                  pl.BlockSpec((n, B_HEADS, B_DK, B_DK), lambda i: (0, 0, 0, 0))],
        out_specs=[pl.BlockSpec((n, c, B_HEADS * B_DK), lambda i: (0, i, 0)),
                   pl.BlockSpec((n, B_HEADS, B_DK, B_DK), lambda i: (0, 0, 0, 0))],
        out_shape=[jax.ShapeDtypeStruct((n, t, B_HEADS * B_DK), F32),
                   jax.ShapeDtypeStruct((n, B_HEADS, B_DK, B_DK), F32)],
        scratch_shapes=[pltpu.VMEM((n, 8 + c, width), F32), pltpu.VMEM((n * B_HEADS, B_DK, B_DK), F32)],
        compiler_params=_cparams("arbitrary"), name="deltanet_prompt",
    )(proj, proj, conv_w, adt, conv_init, s_init)


DN_SAMPLE_SEQS = 8


def _dn_sample_kernel(x_ref, bat_ref, cw_ref, adt_ref, cs_ref, sinit_ref, o_ref, s_ref):
    tn = 4
    hk = B_HEADS * B_DK
    seqs = range(DN_SAMPLE_SEQS)
    xp = jnp.concatenate([cs_ref[...], x_ref[...]], axis=1)
    cw = cw_ref[...]
    y = cw[0] * xp[:, 0:tn]
    for i in range(1, B_CONV):
        y = y + cw[i] * xp[:, i:i + tn]
    cq = y * _sigmoid(y)
    q = cq[:, :, 0:B_HEADS]
    k = cq[:, :, B_HEADS:2 * B_HEADS]
    v = cq[:, :, 2 * B_HEADS:3 * B_HEADS]
    q = q * lax.rsqrt(jnp.sum(q * q, axis=-1, keepdims=True) + EPS) * (B_DK ** -0.5)
    k = k * lax.rsqrt(jnp.sum(k * k, axis=-1, keepdims=True) + EPS)
    bat = bat_ref[...]
    beta = _sigmoid(bat[:, BA_LANE:BA_LANE + B_HEADS, :])
    a = jnp.exp(-jnp.exp(adt_ref[:, 0:1]) * _softplus(bat[:, BA_LANE + B_HEADS:LANES, :] + adt_ref[:, 1:2]))
    lane_head = lax.broadcasted_iota(jnp.int32, (B_HEADS, hk), 1) // B_DK
    head_mask = lane_head == lax.broadcasted_iota(jnp.int32, (B_HEADS, hk), 0)

    def block_diag(x):
        return jnp.where(head_mask, jnp.concatenate([x] * B_HEADS, axis=1), 0.0)

    s_ref[...] = sinit_ref[...]
    for t in range(tn):
        kbd = [block_diag(k[b, t]) for b in seqs]
        ks = [_dot(kbd[b], s_ref[b]) for b in seqs]
        w = [beta[b, :, t:t + 1] * (v[b, t] - a[b, :, t:t + 1] * ks[b]) for b in seqs]
        upd = [_dot_tn(kbd[b], w[b]) for b in seqs]
        for b in seqs:
            for h in range(B_HEADS):
                rows = slice(h * B_DK, (h + 1) * B_DK)
                s_ref[b, rows, :] = s_ref[b, rows, :] * a[b, h:h + 1, t:t + 1] + upd[b][rows, :]
        for b in seqs:
            o_ref[b, t] = _dot(block_diag(q[b, t]), s_ref[b])


def _dn_sample(x4, bat, conv_w, adt_col, conv_state, s_init):
    n = x4.shape[0]
    nb = DN_SAMPLE_SEQS
    rows = 3 * B_HEADS
    hk = B_HEADS * B_DK
    return pl.pallas_call(
        _dn_sample_kernel,
        grid=(n // nb,),
        in_specs=[pl.BlockSpec((nb, 4, rows, B_DK), lambda b: (b, 0, 1, 0)),
                  pl.BlockSpec((nb, LANES, 4), lambda b: (b, 0, 0)),
                  pl.BlockSpec((B_CONV, rows, B_DK), lambda b: (0, 0, 0)),
                  pl.BlockSpec((B_HEADS, 2), lambda b: (0, 0)),
                  pl.BlockSpec((nb, B_CONV - 1, rows, B_DK), lambda b: (b, 0, 0, 0)),
                  pl.BlockSpec((nb, hk, B_DK), lambda b: (b, 0, 0))],
        out_specs=[pl.BlockSpec((nb, 4, B_HEADS, B_DK), lambda b: (b, 0, 0, 0)),
                   pl.BlockSpec((nb, hk, B_DK), lambda b: (b, 0, 0))],
        out_shape=[jax.ShapeDtypeStruct((n, 4, B_HEADS, B_DK), F32),
                   jax.ShapeDtypeStruct((n, hk, B_DK), F32)],
        compiler_params=_cparams("parallel"), name="deltanet_sample",
    )(x4, bat, conv_w, adt_col, conv_state, s_init)


def _gla_kernel(q_ref, k_ref, v_ref, glr_ref, w2_ref, gb_ref, sinit_ref, o_ref, sout_ref,
                s_ref, *, c_real, c, dk, dv, nb):
    ci = pl.program_id(1)
    sub = min(GLA_SUB, c)

    @pl.when(ci == 0)
    def _():
        s_ref[...] = sinit_ref[...]

    def load(ref, b):
        x = ref[b]
        if c_real < c:
            x = jnp.concatenate([x, jnp.zeros((c - c_real, x.shape[1]), F32)], axis=0)
        return x

    row = lax.broadcasted_iota(jnp.int32, (c, c), 0)
    col = lax.broadcasted_iota(jnp.int32, (c, c), 1)
    tri = (row >= col).astype(F32)
    row_valid = lax.broadcasted_iota(jnp.int32, (c, 1), 0) < c_real
    sub_row = lax.broadcasted_iota(jnp.int32, (sub, 1), 0)
    sub_col = lax.broadcasted_iota(jnp.int32, (sub, c), 1)
    key_row = lax.broadcasted_iota(jnp.int32, (c, 1), 0)
    ones = jnp.ones((c, LANES), F32)
    starts = range(0, c, sub)
    seqs = []
    for b in range(nb):
        pre = _dot(load(glr_ref, b), w2_ref[...]) + gb_ref[...]
        la = jnp.where(row_valid, (jnp.minimum(pre, 0.0) - jnp.log(1.0 + jnp.exp(-jnp.abs(pre)))) / C_GATE_TAU, 0.0)
        gc = _dot_f32(tri, la)
        seqs.append((la, gc, load(q_ref, b) * (dk ** -0.5), load(k_ref, b), load(v_ref, b)))
    spans = [gc[lo:lo + 1, :] - gc[lo + sub - 1:lo + sub, :] for _, gc, _, _, _ in seqs for lo in starts]
    safe = jnp.max(functools.reduce(jnp.maximum, spans)) <= GLA_SAFE_SPAN

    def head(b, h):
        la, gc, q, k, v = seqs[b]
        ks = slice(h * dk, (h + 1) * dk)
        return la[:, ks], gc[:, ks], q[:, ks], k[:, ks], v[:, h * dv:(h + 1) * dv]

    for b in range(nb):
        outs = []
        for h in range(C_HEADS):
            la, gc, q, k, v = head(b, h)
            rows = []
            for lo in starts:
                ref_pt = gc[lo:lo + 1, :]
                qsc = q[lo:lo + sub, :] * jnp.exp(gc[lo:lo + sub, :] - ref_pt)
                limit = jnp.where(safe, lo + sub, lo)
                ksc = jnp.where(key_row < limit, k * jnp.exp(jnp.minimum(ref_pt - gc, GLA_SAFE_SPAN)), 0.0)
                rows.append(_dot_nt(qsc, ksc))
            attn = jnp.where(row >= col, jnp.concatenate(rows, axis=0), 0.0)
            s = s_ref[b, h]
            o = _dot(attn, v) + _dot(q * jnp.exp(gc), s)
            glast = gc[c - 1:c, :]
            kd = k * jnp.exp(glast - gc)
            gl_col = jnp.exp(_dot_tn_f32(la, ones))[:, 0:1]
            s_ref[b, h] = s * gl_col + _dot_tn(kd, v)
            outs.append(o[0:c_real, :])
        o_ref[b] = jnp.concatenate(outs, axis=1)

    @pl.when(jnp.logical_not(safe))
    def _():
        for b in range(nb):
            fixes = []
            for h in range(C_HEADS):
                _, gc, q, k, v = head(b, h)
                rows = []
                for lo in starts:
                    qb = q[lo:lo + sub, :]
                    gblk = gc[lo:lo + sub, :]
                    blk = jnp.zeros((sub, c), F32)
                    for jj in range(sub):
                        j = lo + jj
                        e = jnp.exp(jnp.minimum(gblk - gc[j:j + 1, :], 0.0))
                        a = jnp.sum(qb * k[j:j + 1, :] * e, axis=-1, keepdims=True)
                        blk = jnp.where((sub_col == j) & (sub_row >= jj), a, blk)
                    rows.append(blk)
                fixes.append(_dot(jnp.concatenate(rows, axis=0), v)[0:c_real, :])
            o_ref[b] += jnp.concatenate(fixes, axis=1)

    @pl.when(ci == pl.num_programs(1) - 1)
    def _():
        sout_ref[...] = s_ref[...]


def _gla(proj, w2, gb, s_init, nb):
    n, t, _ = proj.shape
    dk, dv = s_init.shape[2], s_init.shape[3]
    c_real = math.gcd(t, GLA_CHUNK)
    c = max(c_real, 8)
    hk, hv = C_HEADS * dk, C_HEADS * dv
    return pl.pallas_call(
        functools.partial(_gla_kernel, c_real=c_real, c=c, dk=dk, dv=dv, nb=nb),
        grid=(n // nb, t // c_real),
        in_specs=[pl.BlockSpec((nb, c_real, hk), lambda b, i: (b, i, 0)),
                  pl.BlockSpec((nb, c_real, hk), lambda b, i: (b, i, 1)),
                  pl.BlockSpec((nb, c_real, hv), lambda b, i: (b, i, 2 * hk // hv)),
                  pl.BlockSpec((nb, c_real, LANES), lambda b, i: (b, i, (2 * hk + 2 * hv) // LANES)),
                  pl.BlockSpec((LANES, hk), lambda b, i: (0, 0)),
                  pl.BlockSpec((1, hk), lambda b, i: (0, 0)),
                  pl.BlockSpec((nb, C_HEADS, dk, dv), lambda b, i: (b, 0, 0, 0))],
        out_specs=[pl.BlockSpec((nb, c_real, hv), lambda b, i: (b, i, 0)),
                   pl.BlockSpec((nb, C_HEADS, dk, dv), lambda b, i: (b, 0, 0, 0))],
        out_shape=[jax.ShapeDtypeStruct((n, t, hv), F32),
                   jax.ShapeDtypeStruct((n, C_HEADS, dk, dv), F32)],
        scratch_shapes=[pltpu.VMEM((nb, C_HEADS, dk, dv), F32)],
        compiler_params=_cparams("parallel", "arbitrary"), name="gla",
    )(proj, proj, proj, proj, w2, gb, s_init)


def _gated_headnorm(o, z, nw, heads, width):
    parts = []
    for h in range(heads):
        zz = z[:, h * width:(h + 1) * width]
        parts.append(_rms(o[:, h * width:(h + 1) * width], nw) * (zz * _sigmoid(zz)))
    return jnp.concatenate(parts, axis=1)


def _row_parts(tm):
    tp = tm // MIX_PARTS if tm % (8 * MIX_PARTS) == 0 else tm
    return [slice(lo, lo + tp) for lo in range(0, tm, tp)]


def _mix_even_kernel(oa_ref, ob_ref, z_ref, nw_ref, w_ref, r_ref, g_ref, out_ref):
    ka = A_HEADS * HEAD_DIM
    for rows in _row_parts(out_ref.shape[0]):
        obn = _gated_headnorm(ob_ref[rows, :], z_ref[rows, :], nw_ref[...], B_HEADS, B_DK)
        m = _dot(oa_ref[rows, :], w_ref[0:ka, :]) + _dot(obn, w_ref[ka:, :])
        out_ref[rows, :] = r_ref[rows, :] + _rms(m, g_ref[...])


def _mix_even(o_a, o_b, proj, nw, w_out, r, g_post, tm):
    m, d = r.shape
    ka = o_a.shape[1]
    kb = o_b.shape[1]
    z_block = (2 * 3 * A_HEADS * HEAD_DIM) // kb
    return pl.pallas_call(
        _mix_even_kernel, grid=(m // tm,),
        in_specs=[pl.BlockSpec((tm, ka), lambda i: (i, 0)),
                  pl.BlockSpec((tm, kb), lambda i: (i, 0)),
                  pl.BlockSpec((tm, kb), lambda i: (i, z_block)),
                  pl.BlockSpec((1, B_DK), lambda i: (0, 0)),
                  pl.BlockSpec((ka + kb, d), lambda i: (0, 0)),
                  pl.BlockSpec((tm, d), lambda i: (i, 0)),
                  pl.BlockSpec((1, d), lambda i: (0, 0))],
        out_specs=pl.BlockSpec((tm, d), lambda i: (i, 0)),
        out_shape=jax.ShapeDtypeStruct((m, d), F32),
        compiler_params=_cparams("parallel"), name="mix_even",
    )(o_a, o_b, proj, nw.reshape(1, -1), w_out, r, g_post.reshape(1, d))


def _mix_odd_kernel(o_ref, z_ref, nw_ref, w_ref, r_ref, g_ref, out_ref, *, dv):
    for rows in _row_parts(out_ref.shape[0]):
        on = _gated_headnorm(o_ref[rows, :], z_ref[rows, :], nw_ref[...], C_HEADS, dv)
        out_ref[rows, :] = r_ref[rows, :] + _rms(_dot(on, w_ref[...]), g_ref[...])


def _mix_odd(o, proj, nw, w_out, r, g_post, tm):
    m, d = r.shape
    kv = o.shape[1]
    return pl.pallas_call(
        functools.partial(_mix_odd_kernel, dv=kv // C_HEADS), grid=(m // tm,),
        in_specs=[pl.BlockSpec((tm, kv), lambda i: (i, 0)),
                  pl.BlockSpec((tm, kv), lambda i: (i, 2)),
                  pl.BlockSpec((1, kv // C_HEADS), lambda i: (0, 0)),
                  pl.BlockSpec((kv, d), lambda i: (0, 0)),
                  pl.BlockSpec((tm, d), lambda i: (i, 0)),
                  pl.BlockSpec((1, d), lambda i: (0, 0))],
        out_specs=pl.BlockSpec((tm, d), lambda i: (i, 0)),
        out_shape=jax.ShapeDtypeStruct((m, d), F32),
        compiler_params=_cparams("parallel"), name="mix_odd",
    )(o, proj, nw.reshape(1, -1), w_out, r, g_post.reshape(1, d))


def _ffn_kernel(x_ref, gpre_ref, wg_ref, wv_ref, cwg_ref, cwv_ref, wd_ref, gpost_ref, ig_ref, iv_ref,
                o_ref, tg_ref, tv_ref, h_ref, ug, uv, *carry, tm, halo, shift):
    i = pl.program_id(1)
    j = pl.program_id(2)

    @pl.when(j == 0)
    def _():
        h_ref[...] = _rms(x_ref[0], gpre_ref[...]).astype(BF16)
        o_ref[0] = jnp.zeros_like(o_ref[0])

    if carry:
        cg, cv = carry

        @pl.when(i == 0)
        def _():
            ug[0:halo, :] = ig_ref[0]
            uv[0:halo, :] = iv_ref[0]

        @pl.when(i > 0)
        def _():
            ug[0:halo, :] = cg[j]
            uv[0:halo, :] = cv[j]
    else:
        ug[0:halo, :] = ig_ref[0]
        uv[0:halo, :] = iv_ref[0]

    tf = ug.shape[1]
    sub = min(FFN_SUB, tf)
    cols = [slice(s * sub, (s + 1) * sub) for s in range(tf // sub)]
    for cs in cols:
        ug[halo:halo + tm, cs] = jnp.dot(h_ref[...], wg_ref[:, cs], preferred_element_type=F32)
        uv[halo:halo + tm, cs] = jnp.dot(h_ref[...], wv_ref[:, cs], preferred_element_type=F32)

    def conv(u, cw_ref, cs):
        y = cw_ref[0:1, cs] * u[pl.ds(halo - (FFN_CONV - 1) * shift, tm), cs]
        for tap in range(1, FFN_CONV):
            y = y + cw_ref[tap:tap + 1, cs] * u[pl.ds(halo - (FFN_CONV - 1 - tap) * shift, tm), cs]
        return y

    acc = None
    for cs in cols:
        act = _gelu_tanh(conv(ug, cwg_ref, cs)) * conv(uv, cwv_ref, cs)
        part = _dot(act, wd_ref[cs, :])
        acc = part if acc is None else acc + part
    tail_g = ug[tm:tm + halo, :]
    tail_v = uv[tm:tm + halo, :]
    tg_ref[0, 0] = tail_g
    tv_ref[0, 0] = tail_v
    if carry:
        cg[j] = tail_g
        cv[j] = tail_v
    o_ref[0] += acc

    @pl.when(j == pl.num_programs(2) - 1)
    def _():
        o_ref[0] = x_ref[0] + _rms(o_ref[0], gpost_ref[...])


def _ffn(x, g_pre, w_up, conv_w, w_down, g_post, init, layer, tm, tf, shift):
    n, t, d = x.shape
    dff = w_down.shape[1]
    halo = init.shape[1]
    ni, nj = t // tm, dff // tf
    scratch = [pltpu.VMEM((tm, d), BF16), pltpu.VMEM((halo + tm, tf), F32), pltpu.VMEM((halo + tm, tf), F32)]
    if ni > 1:
        scratch += [pltpu.VMEM((nj, halo, tf), F32), pltpu.VMEM((nj, halo, tf), F32)]
    out, tail_g, tail_v = pl.pallas_call(
        functools.partial(_ffn_kernel, tm=tm, halo=halo, shift=shift),
        grid=(n, ni, nj),
        in_specs=[pl.BlockSpec((1, tm, d), lambda b, i, j: (b, i, 0)),
                  pl.BlockSpec((1, d), lambda b, i, j: (0, 0)),
                  pl.BlockSpec((None, d, tf), lambda b, i, j: (layer, 0, j)),
                  pl.BlockSpec((None, d, tf), lambda b, i, j: (layer, 0, j + nj)),
                  pl.BlockSpec((None, FFN_CONV, tf), lambda b, i, j: (layer, 0, j)),
                  pl.BlockSpec((None, FFN_CONV, tf), lambda b, i, j: (layer, 0, j + nj)),
                  pl.BlockSpec((None, tf, d), lambda b, i, j: (layer, j, 0)),
                  pl.BlockSpec((1, d), lambda b, i, j: (0, 0)),
                  pl.BlockSpec((1, halo, tf), lambda b, i, j: (b, 0, j)),
                  pl.BlockSpec((1, halo, tf), lambda b, i, j: (b, 0, j + nj))],
        out_specs=[pl.BlockSpec((1, tm, d), lambda b, i, j: (b, i, 0), pipeline_mode=pl.Buffered(1)),
                   pl.BlockSpec((1, 1, halo, tf), lambda b, i, j: (b, i, 0, j)),
                   pl.BlockSpec((1, 1, halo, tf), lambda b, i, j: (b, i, 0, j))],
        out_shape=[jax.ShapeDtypeStruct((n, t, d), F32),
                   jax.ShapeDtypeStruct((n, ni, halo, dff), F32),
                   jax.ShapeDtypeStruct((n, ni, halo, dff), F32)],
        scratch_shapes=scratch,
        compiler_params=_cparams("parallel", "arbitrary", "arbitrary"), name="conv_ffn",
    )(x, g_pre.reshape(1, d), w_up, w_up, conv_w, conv_w, w_down, g_post.reshape(1, d), init, init)
    return out, jnp.concatenate([tail_g[:, ni - 1], tail_v[:, ni - 1]], axis=-1)


def _ple_kernel(r_ref, p_ref, wp_ref, wg_ref, o_ref):
    r = r_ref[...]
    o_ref[...] = r + _dot(p_ref[...], wp_ref[...]) * _sigmoid(_dot(r, wg_ref[...]))


def _ple(r, p, w_proj, w_gate, layer, tm):
    m, d = r.shape
    pd = p.shape[1]
    return pl.pallas_call(
        _ple_kernel, grid=(m // tm,),
        in_specs=[pl.BlockSpec((tm, d), lambda i: (i, 0)),
                  pl.BlockSpec((tm, pd), lambda i: (i, 0)),
                  pl.BlockSpec((None, pd, d), lambda i: (layer, 0, 0)),
                  pl.BlockSpec((None, d, d), lambda i: (layer, 0, 0))],
        out_specs=pl.BlockSpec((tm, d), lambda i: (i, 0)),
        out_shape=jax.ShapeDtypeStruct((m, d), F32),
        compiler_params=_cparams("parallel"), name="ple",
    )(r, p, w_proj, w_gate)


def _reorder_cols_kernel(wt_ref, o_ref, *, spans):
    col = 0
    for span in spans:
        if isinstance(span, int):
            o_ref[:, col:col + span] = jnp.zeros((o_ref.shape[0], span), BF16)
            col += span
        else:
            a, b = span
            rows = -(-(b - a) // LANES) * LANES
            piece = wt_ref[a:a + rows, :].T
            o_ref[:, col:col + b - a] = piece[:, 0:b - a].astype(BF16)
            col += b - a


def _reorder_cols(w, spans):
    _, k, n_in = w.shape
    n_out = sum(s if isinstance(s, int) else s[1] - s[0] for s in spans)
    assert all(isinstance(s, int) or s[0] % 8 == 0 and s[0] + -(-(s[1] - s[0]) // LANES) * LANES <= n_in for s in spans)
    tr = WEIGHT_PREP_ROWS
    return pl.pallas_call(
        functools.partial(_reorder_cols_kernel, spans=spans),
        grid=(k // tr,),
        in_specs=[pl.BlockSpec((None, n_in, tr), lambda i: (0, 0, i))],
        out_specs=pl.BlockSpec((tr, n_out), lambda i: (i, 0)),
        out_shape=jax.ShapeDtypeStruct((k, n_out), BF16),
        compiler_params=_cparams("parallel"), name="weight_prep",
    )(jnp.swapaxes(w, 1, 2))


def _prep_weights(w_in_even, w_out_even, w_in_odd, w_out_odd, gla_gate_w2, ffn_w_up, ffn_w_down,
                  ple_w_proj, ple_w_gate, dn_a_log, dn_dt_bias):
    ab = 3 * A_HEADS * HEAD_DIM + 3 * B_HEADS * B_DK
    nl = 2 * B_HEADS
    n0 = w_in_even.shape[2]
    gap = -n0 % PROJ0_TN
    assert gap >= LANES - nl
    w0 = _reorder_cols(w_in_even, [(0, ab), (ab + nl, n0), gap, (ab, ab + nl)])
    d = w_in_odd.shape[1]
    qkv = 2 * (d // 2) + d
    n1 = w_in_odd.shape[2]
    w1 = _reorder_cols(w_in_odd, [(0, qkv), (qkv + C_GATE_RANK, n1), (qkv, qkv + C_GATE_RANK), -n1 % PROJ_TN])
    w2 = jnp.pad(gla_gate_w2[0], ((0, LANES - C_GATE_RANK), (0, 0))).astype(BF16)
    adt_row = jnp.pad(jnp.stack([dn_a_log[0], dn_dt_bias[0]]), ((0, 0), (BA_LANE + B_HEADS, 0)))
    adt_col = jnp.stack([dn_a_log[0], dn_dt_bias[0]], axis=1)
    return dict(w0=w0, w1=w1, w2=w2, adt_row=adt_row, adt_col=adt_col,
                w_out_even=w_out_even[0].astype(BF16), w_out_odd=w_out_odd[0].astype(BF16),
                w_up=ffn_w_up.astype(BF16), w_down=ffn_w_down.astype(BF16),
                w_proj=ple_w_proj.astype(BF16), w_gate=ple_w_gate.astype(BF16))


def _layer_tail(r, p, layer, wt, prm, ffn_init, n, t, tm):
    d = r.shape[1]
    r, tail = _ffn(r.reshape(n, t, d), prm["norm_ffn_pre"][layer], wt["w_up"], prm["ffn_conv_w"],
                   wt["w_down"], prm["norm_ffn_post"][layer], ffn_init, layer, PROMPT_FFN_TM, PROMPT_FFN_TF, 1)
    r = _ple(r.reshape(n * t, d), p, wt["w_proj"], wt["w_gate"], layer, tm)
    return r, tail


def _prompt_group(x, p, wt, prm):
    n, t, d = x.shape
    m = n * t
    tm = MIX_TM
    dff2 = prm["ffn_conv_w"].shape[-1]
    r = x.reshape(m, d)
    proj = _rms_matmul(r, prm["norm_mix_pre"][0], wt["w0"], PROMPT_PROJ_TM, PROJ0_TN)
    pv = proj.reshape(n, t, -1)
    o_a = _attn_prompt(pv, n, t).reshape(m, -1)
    width = 3 * B_HEADS * B_DK
    o_b, dn_state = _dn_prompt(pv, prm["dn_conv_w"][0], wt["adt_row"], jnp.zeros((n, 8, width), F32),
                               jnp.zeros((n, B_HEADS, B_DK, B_DK), F32))
    r = _mix_even(o_a, o_b.reshape(m, -1), proj, prm["dn_norm_w"][0], wt["w_out_even"], r,
                  prm["norm_mix_post"][0], tm)
    keep = min(A_WMAX, t)
    hd = A_HEADS * HEAD_DIM
    win_k = pv[:, t - keep:, hd:2 * hd].reshape(n, keep, A_HEADS, HEAD_DIM)
    win_v = pv[:, t - keep:, 2 * hd:3 * hd].reshape(n, keep, A_HEADS, HEAD_DIM)
    dn_conv = pv[:, t - (B_CONV - 1):, width:2 * width]
    ffn_zero = jnp.zeros((n, 8, dff2), F32)
    r, tail0 = _layer_tail(r, p[0].reshape(m, -1), 0, wt, prm, ffn_zero, n, t, tm)
    proj1 = _rms_matmul(r, prm["norm_mix_pre"][1], wt["w1"], PROMPT_PROJ_TM, PROJ_TN)
    o_c, gla_state = _gla(proj1.reshape(n, t, -1), wt["w2"], prm["gla_gate_b"][0].reshape(1, -1),
                          jnp.zeros((n, C_HEADS, d // (2 * C_HEADS), d // C_HEADS), F32), n)
    r = _mix_odd(o_c.reshape(m, -1), proj1, prm["gla_norm_w"][0], wt["w_out_odd"], r, prm["norm_mix_post"][1], tm)
    r, tail1 = _layer_tail(r, p[1].reshape(m, -1), 1, wt, prm, ffn_zero, n, t, tm)
    ffn_conv = jnp.stack([tail0[:, 8 - (FFN_CONV - 1):], tail1[:, 8 - (FFN_CONV - 1):]])
    return (r.reshape(n, t, d), win_k[None], win_v[None], dn_conv[None], dn_state[None], gla_state[None], ffn_conv)


def _sample_group(x, p, cache_k, cache_v, dn_conv_state, dn_state, gla_state, ffn_state, wt, prm):
    n, t, d = x.shape
    m = n * t
    tm = MIX_TM
    r = x.reshape(m, d)

    def time_major(a):
        return a.reshape(n, t, -1).transpose(1, 0, 2).reshape(1, m, -1)

    def seq_major(a):
        return a.reshape(t, n, -1).transpose(1, 0, 2).reshape(m, -1)

    def ffn_layer(r, layer):
        init = ffn_state[layer].transpose(1, 0, 2).reshape(1, (FFN_CONV - 1) * n, -1)
        rt, tail = _ffn(time_major(r), prm["norm_ffn_pre"][layer], wt["w_up"], prm["ffn_conv_w"],
                        wt["w_down"], prm["norm_ffn_post"][layer], init, layer, m, SAMPLE_FFN_TF, n)
        r = _ple(seq_major(rt), p[layer].reshape(m, -1), wt["w_proj"], wt["w_gate"], layer, tm)
        return r, tail.reshape(FFN_CONV - 1, n, -1).transpose(1, 0, 2)

    proj = _rms_matmul(r, prm["norm_mix_pre"][0], wt["w0"], m, PROJ0_TN)
    width = 3 * B_HEADS * B_DK
    heads = proj.reshape(n, t, -1, HEAD_DIM)
    o_a = _attn_sample(heads, cache_k[0], cache_v[0])
    bat = heads[:, :, heads.shape[2] - 1].transpose(0, 2, 1)
    o_b, dn_new = _dn_sample(heads, bat, prm["dn_conv_w"][0].reshape(B_CONV, -1, B_DK), wt["adt_col"],
                             dn_conv_state[0].reshape(n, B_CONV - 1, -1, B_DK),
                             dn_state[0].reshape(n, B_HEADS * B_DK, B_DK))
    r = _mix_even(o_a.reshape(m, -1), o_b.reshape(m, -1), proj, prm["dn_norm_w"][0], wt["w_out_even"], r,
                  prm["norm_mix_post"][0], tm)
    win_k = heads[:, :, A_HEADS:2 * A_HEADS]
    win_v = heads[:, :, 2 * A_HEADS:3 * A_HEADS]
    dn_conv = heads[:, t - (B_CONV - 1):, 3 * A_HEADS:3 * (A_HEADS + B_HEADS)].reshape(n, B_CONV - 1, width)
    r, tail0 = ffn_layer(r, 0)
    proj1 = _rms_matmul(r, prm["norm_mix_pre"][1], wt["w1"], m, PROJ_TN)
    o_c, gla_new = _gla(proj1.reshape(n, t, -1), wt["w2"], prm["gla_gate_b"][0].reshape(1, -1), gla_state[0],
                        SAMPLE_GLA_SEQS)
    r = _mix_odd(o_c.reshape(m, -1), proj1, prm["gla_norm_w"][0], wt["w_out_odd"], r, prm["norm_mix_post"][1], tm)
    r, tail1 = ffn_layer(r, 1)
    return (r.reshape(n, t, d), win_k[None], win_v[None], dn_conv[None],
            dn_new.reshape(dn_state.shape), gla_new[None], jnp.stack([tail0, tail1]))


def kernel(x_prompt, x_sample, cache_win_k, cache_win_v, state_dn_conv, state_dn, state_gla, state_ffn_conv, p_prompt, p_sample, norm_mix_pre, norm_mix_post, norm_ffn_pre, norm_ffn_post, w_in_even, w_out_even, dn_conv_w, dn_a_log, dn_dt_bias, dn_norm_w, w_in_odd, gla_gate_w2, gla_gate_b, gla_norm_w, w_out_odd, ffn_w_up, ffn_conv_w, ffn_w_down, ple_w_proj, ple_w_gate):
    wt = _prep_weights(w_in_even, w_out_even, w_in_odd, w_out_odd, gla_gate_w2, ffn_w_up, ffn_w_down,
                       ple_w_proj, ple_w_gate, dn_a_log, dn_dt_bias)
    prm = dict(norm_mix_pre=norm_mix_pre, norm_mix_post=norm_mix_post, norm_ffn_pre=norm_ffn_pre,
               norm_ffn_post=norm_ffn_post, dn_conv_w=dn_conv_w, dn_norm_w=dn_norm_w, gla_gate_b=gla_gate_b,
               gla_norm_w=gla_norm_w, ffn_conv_w=ffn_conv_w)
    yp, kp, vp, dcp, dsp, gsp, fcp = _prompt_group(x_prompt, p_prompt, wt, prm)
    ys, ks, vs, dcs, dss, gss, fcs = _sample_group(x_sample, p_sample, cache_win_k, cache_win_v, state_dn_conv,
                                                   state_dn, state_gla, state_ffn_conv, wt, prm)
    return (yp, ys, kp, vp, dcp, dsp, gsp, fcp, ks, vs, dcs, dss, gss, fcs)
```
